```python
import math
import jax, jax.numpy as jnp
from jax import lax
import numpy as np

D_MODEL = 2048
BATCH = 16
SEQ = 2048
DEPTH = 2

GRID_W = 64
CTX_LEN = 256
N_MIXERS = 2
N_LRU = (DEPTH + 1) // 2
N_ATT = DEPTH // 2
LRU_WIDTH = D_MODEL
LRU_BLOCKS = 16
LRU_BLOCK = LRU_WIDTH // LRU_BLOCKS
LRU_C = 8.0
CONV_W = 4
CONV_PAD = (2, 1)
ATT_HEAD_DIM = 64
ATT_HEADS = D_MODEL // (2 * ATT_HEAD_DIM)
ATT_V_DIM = 2 * ATT_HEAD_DIM
ATT_WIDTH = ATT_HEADS * ATT_V_DIM
ROPE_FREQS = ATT_HEAD_DIM // 4
ROPE_BASE = 10000.0
Q_BLOCK = 128
EPS = 1e-6

kernel_name = "hybrid_rglru_diffattn_dit_block"


def rms_norm(x, g):
    xf = x.astype(jnp.float32)
    y = xf * lax.rsqrt(jnp.mean(xf * xf, axis=-1, keepdims=True) + EPS)
    return (y * g.astype(jnp.float32)).astype(x.dtype)


def dwconv(u, w, b):
    y = lax.conv_general_dilated(u, w[:, None, :], window_strides=(1,), padding=[CONV_PAD],
                                 dimension_numbers=("NWC", "WIO", "NWC"),
                                 feature_group_count=u.shape[-1])
    return y + b


def lru_coeffs(xc, gw, gb, lam):
    B_, L, _ = xc.shape
    xb = xc.reshape(B_, L, LRU_BLOCKS, LRU_BLOCK)
    gates = jax.nn.sigmoid((jnp.einsum("blnc,gncd->gblnd", xb, gw) + gb[:, None, None]).astype(jnp.float32))
    gates = gates.reshape(2, B_, L, LRU_WIDTH)
    r, i = gates[0], gates[1]
    log_a = -LRU_C * r * jax.nn.softplus(-lam.astype(jnp.float32))
    a = jnp.exp(log_a)
    b = jnp.sqrt(-jnp.expm1(2.0 * log_a)) * (i * xc.astype(jnp.float32))
    return a, b


def linear_scan(a, b, h0, reverse):
    def step(h, ab):
        at, bt = ab
        h = at * h + bt
        return h, h
    h_end, hs = lax.scan(step, h0, (jnp.swapaxes(a, 0, 1), jnp.swapaxes(b, 0, 1)), reverse=reverse)
    return h_end, jnp.swapaxes(hs, 0, 1)


def rglru_mixer(h_lat, h_ctx, w_in, conv_w, conv_b, gate_w, gate_b, lam, w_out, need_ctx_out):
    u_lat, g_lat = jnp.split(h_lat @ w_in, 2, axis=-1)
    u_ctx, g_ctx = jnp.split(h_ctx @ w_in, 2, axis=-1)
    xc_lat = dwconv(u_lat, conv_w, conv_b)
    xc_ctx = dwconv(u_ctx, conv_w, conv_b)
    h0 = jnp.zeros((h_lat.shape[0], LRU_WIDTH), jnp.float32)
    outs_lat, outs_ctx = [], []
    for d, rev in enumerate((False, True)):
        a_c, b_c = lru_coeffs(xc_ctx, gate_w[d], gate_b[d], lam[d])
        h_end, hs_ctx = linear_scan(a_c, b_c, h0, rev)
        a_l, b_l = lru_coeffs(xc_lat, gate_w[d], gate_b[d], lam[d])
        _, hs_lat = linear_scan(a_l, b_l, h_end, rev)
        outs_lat.append(hs_lat)
        outs_ctx.append(hs_ctx)
    y_lat = ((outs_lat[0] + outs_lat[1]).astype(h_lat.dtype) * jax.nn.silu(g_lat)) @ w_out
    y_ctx = None
    if need_ctx_out:
        y_ctx = ((outs_ctx[0] + outs_ctx[1]).astype(h_ctx.dtype) * jax.nn.silu(g_ctx)) @ w_out
    return y_lat, y_ctx


def axial_rope_tables(S):
    rows = S // GRID_W
    row = jnp.broadcast_to(jnp.arange(rows)[:, None], (rows, GRID_W)).reshape(S).astype(jnp.float32)
    col = jnp.broadcast_to(jnp.arange(GRID_W)[None, :], (rows, GRID_W)).reshape(S).astype(jnp.float32)
    inv = ROPE_BASE ** (-jnp.arange(ROPE_FREQS, dtype=jnp.float32) / ROPE_FREQS)
    ang = jnp.stack([row[:, None] * inv, col[:, None] * inv], axis=1)
    return jnp.cos(ang), jnp.sin(ang)


def apply_axial_rope(x, cos, sin):
    xs = x.astype(jnp.float32).reshape(x.shape[:-1] + (2, 2, ROPE_FREQS))
    x1, x2 = xs[..., 0, :], xs[..., 1, :]
    out = jnp.stack([x1 * cos - x2 * sin, x2 * cos + x1 * sin], axis=-2)
    return out.reshape(x.shape).astype(x.dtype)


def diff_attend(q, k, v, lam):
    s = jnp.einsum("bhcqd,bhckd->bhcqk", q, k).astype(jnp.float32)
    p = jax.nn.softmax(s, axis=-1)
    attn = p[:, :, 0] - lam * p[:, :, 1]
    return jnp.einsum("bhqk,bhkv->bhqv", attn.astype(v.dtype), v)


def diff_attn_mixer(h_lat, h_ctx, w_in, q_g, k_g, lam_vecs, subln_g, w_out, lam_init, cos, sin, need_ctx_out):
    def project(h):
        B_, L, _ = h.shape
        q, k, v, g = jnp.split(h @ w_in, 4, axis=-1)
        q = rms_norm(q.reshape(B_, L, ATT_HEADS, 2, ATT_HEAD_DIM), q_g).transpose(0, 2, 3, 1, 4)
        k = rms_norm(k.reshape(B_, L, ATT_HEADS, 2, ATT_HEAD_DIM), k_g).transpose(0, 2, 3, 1, 4)
        v = v.reshape(B_, L, ATT_HEADS, ATT_V_DIM).transpose(0, 2, 1, 3)
        return q * (ATT_HEAD_DIM ** -0.5), k, v, g

    def finish(o, g):
        B_, _, L, _ = o.shape
        o = rms_norm(o, subln_g) * (1.0 - lam_init)
        o = o.transpose(0, 2, 1, 3).reshape(B_, L, ATT_WIDTH)
        return (o * jax.nn.silu(g)) @ w_out

    lv = lam_vecs.astype(jnp.float32)
    lam = jnp.exp(jnp.sum(lv[0] * lv[1])) - jnp.exp(jnp.sum(lv[2] * lv[3])) + lam_init

    q_l, k_l, v_l, g_l = project(h_lat)
    q_c, k_c, v_c, g_c = project(h_ctx)
    q_l = apply_axial_rope(q_l, cos, sin)
    k_l = apply_axial_rope(k_l, cos, sin)
    k_all = jnp.concatenate([k_l, k_c], axis=3)
    v_all = jnp.concatenate([v_l, v_c], axis=2)

    B_, H, _, S, d = q_l.shape
    nblk = S // Q_BLOCK
    qb = jnp.moveaxis(q_l.reshape(B_, H, 2, nblk, Q_BLOCK, d), 3, 0)
    ob = lax.map(lambda qq: diff_attend(qq, k_all, v_all, lam), qb)
    o_l = jnp.moveaxis(ob, 0, 2).reshape(B_, H, S, ATT_V_DIM)
    y_lat = finish(o_l, g_l)
    y_ctx = None
    if need_ctx_out:
        y_ctx = finish(diff_attend(q_c, k_c, v_c, lam), g_c)
    return y_lat, y_ctx


def setup_inputs(seed: int = 0) -> dict:
    key = jax.random.key(seed)
    ks = jax.random.split(key, 24)
    f32 = jnp.float32
    D, W = D_MODEL, LRU_WIDTH
    u = jax.random.uniform(ks[11], (N_LRU, 2, W), f32, 0.9, 0.999)
    a = u ** (1.0 / LRU_C)
    return {
        "x": jax.random.normal(ks[0], (BATCH, SEQ, D), f32),
        "c": jax.random.normal(ks[1], (BATCH, D), f32),
        "ctx": jax.random.normal(ks[2], (BATCH, CTX_LEN, D), f32),
        "c_ctx": jax.random.normal(ks[3], (D,), f32),
        "mod_w": jax.random.normal(ks[4], (DEPTH, D, 3 * D), f32) * (0.5 * D ** -0.5),
        "mod_b": jax.random.normal(ks[5], (DEPTH, 3 * D), f32) * 0.01,
        "norm_g": 1.0 + 0.1 * jax.random.normal(ks[6], (DEPTH, D), f32),
        "lru_w_in": jax.random.normal(ks[7], (N_LRU, D, 2 * W), f32) * D ** -0.5,
        "lru_conv_w": jax.random.normal(ks[8], (N_LRU, CONV_W, W), f32) * CONV_W ** -0.5,
        "lru_conv_b": jax.random.normal(ks[9], (N_LRU, W), f32) * 0.01,
        "lru_gate_w": jax.random.normal(ks[10], (N_LRU, 2, 2, LRU_BLOCKS, LRU_BLOCK, LRU_BLOCK), f32) * LRU_BLOCK ** -0.5,
        "lru_gate_b": jax.random.normal(ks[12], (N_LRU, 2, 2, LRU_BLOCKS, LRU_BLOCK), f32) * 0.01,
        "lru_lambda": jnp.log(a) - jnp.log1p(-a),
        "lru_w_out": jax.random.normal(ks[13], (N_LRU, W, D), f32) * W ** -0.5,
        "att_w_in": jax.random.normal(ks[14], (N_ATT, D, 4 * ATT_WIDTH), f32) * D ** -0.5,
        "att_q_norm": 1.0 + 0.1 * jax.random.normal(ks[15], (N_ATT, ATT_HEAD_DIM), f32),
        "att_k_norm": 1.0 + 0.1 * jax.random.normal(ks[16], (N_ATT, ATT_HEAD_DIM), f32),
        "att_lambda": 0.1 * jax.random.normal(ks[17], (N_ATT, 4, ATT_HEAD_DIM), f32),
        "att_subln": 1.0 + 0.1 * jax.random.normal(ks[18], (N_ATT, ATT_V_DIM), f32),
        "att_w_out": jax.random.normal(ks[19], (N_ATT, ATT_WIDTH, D), f32) * ATT_WIDTH ** -0.5,
    }


def reference(x, c, ctx, c_ctx, mod_w, mod_b, norm_g, lru_w_in, lru_conv_w, lru_conv_b, lru_gate_w,
              lru_gate_b, lru_lambda, lru_w_out, att_w_in, att_q_norm, att_k_norm, att_lambda,
              att_subln, att_w_out):
    S = x.shape[1]
    cos, sin = axial_rope_tables(S)
    sc = jax.nn.silu(c)
    scc = jax.nn.silu(c_ctx)
    for i in range(DEPTH):
        need_ctx_out = i < DEPTH - 1
        shift, scale, gate = jnp.split(sc @ mod_w[i] + mod_b[i], 3, axis=-1)
        cshift, cscale, cgate = jnp.split(scc @ mod_w[i] + mod_b[i], 3, axis=-1)
        h_lat = rms_norm(x, norm_g[i]) * (1.0 + scale[:, None]) + shift[:, None]
        h_ctx = rms_norm(ctx, norm_g[i]) * (1.0 + cscale) + cshift
        j = i // N_MIXERS
        if i % N_MIXERS == 0:
            y_lat, y_ctx = rglru_mixer(h_lat, h_ctx, lru_w_in[j], lru_conv_w[j], lru_conv_b[j],
                                       lru_gate_w[j], lru_gate_b[j], lru_lambda[j], lru_w_out[j],
                                       need_ctx_out)
        else:
            lam_init = 0.8 - 0.6 * math.exp(-0.3 * i)
            y_lat, y_ctx = diff_attn_mixer(h_lat, h_ctx, att_w_in[j], att_q_norm[j], att_k_norm[j],
                                           att_lambda[j], att_subln[j], att_w_out[j], lam_init,
                                           cos, sin, need_ctx_out)
        x = x + gate[:, None] * y_lat
        if need_ctx_out:
            ctx = ctx + cgate * y_ctx
    return x
```

```python
import functools
import math

import jax
import jax.numpy as jnp
from jax import lax
from jax.experimental import pallas as pl
from jax.experimental.pallas import tpu as pltpu

F32 = jnp.float32
BF16 = jnp.bfloat16

EPS = 1e-6
LRU_C = 8.0
GRID_W = 64
HEAD_DIM = 64
ROPE_FREQS = HEAD_DIM // 4
ROPE_BASE = 10000.0
LANES = 128
MXU_DIM = 256
VMEM_LIMIT = 56 * 1024 * 1024


def _params(sem, vmem=VMEM_LIMIT):
    return pltpu.CompilerParams(dimension_semantics=sem, vmem_limit_bytes=vmem)


def _sigmoid(x):
    return 1.0 / (1.0 + jnp.exp(-x))


def _silu(x):
    return x * _sigmoid(x)


def _dot(a, b):
    return jnp.dot(a, b, preferred_element_type=F32)


def _resident(shape):
    nd = len(shape)
    return pl.BlockSpec(shape, lambda *_: (0,) * nd, pipeline_mode=pl.Buffered(1))


def _mod_kernel(cs_ref, w_ref, b_ref, o_ref):
    a = _silu(cs_ref[...])
    w = w_ref[0]
    a_hi = a.astype(BF16)
    a_lo = (a - a_hi.astype(F32)).astype(BF16)
    w_hi = w.astype(BF16)
    w_lo = (w - w_hi.astype(F32)).astype(BF16)
    acc = _dot(a_hi, w_hi) + _dot(a_lo, w_hi) + _dot(a_hi, w_lo)
    o_ref[0] = acc + b_ref[0]


def _modulation(cs, mod_w, mod_b):
    depth, d, n3 = mod_w.shape
    rows = cs.shape[0]
    tn = 768 if n3 % 768 == 0 else n3
    return pl.pallas_call(
        _mod_kernel,
        grid=(depth, n3 // tn),
        in_specs=[
            pl.BlockSpec((rows, d), lambda i, n: (0, 0)),
            pl.BlockSpec((1, d, tn), lambda i, n: (i, 0, n)),
            pl.BlockSpec((1, 1, tn), lambda i, n: (i, 0, n)),
        ],
        out_specs=pl.BlockSpec((1, rows, tn), lambda i, n: (i, 0, n)),
        out_shape=jax.ShapeDtypeStruct((depth, rows, n3), F32),
        compiler_params=_params(("parallel", "parallel")),
        name="modulation",
    )(cs, mod_w, mod_b.reshape(depth, 1, n3))


def _norm_mod(xb, g, scale, shift):
    ms = jnp.mean(xb * xb, axis=-1, keepdims=True)
    return (xb * lax.rsqrt(ms + EPS) * g) * (1.0 + scale) + shift


def _lru_in_kernel(ctx_ref, x_ref, sh_ref, sc_ref, g_ref, w_ref, u_ref, gs_ref, slab_ref, *, nc, nb, tt, d, w, nch):
    is_ctx = pl.program_id(0) < nc
    g = g_ref[...]
    for b in range(nb):
        xb = jnp.where(is_ctx, ctx_ref[b], x_ref[b])
        h = _norm_mod(xb, g, sc_ref[0, b:b + 1, :], sh_ref[0, b:b + 1, :])
        for s in range(d // LANES):
            slab_ref[s, pl.ds(b, tt, stride=nb), :] = h[:, s * LANES:(s + 1) * LANES]
    hfull = jnp.concatenate([slab_ref[s] for s in range(d // LANES)], axis=1).astype(BF16)
    for n in range(0, 2 * w, nch):
        res = _dot(hfull, w_ref[:, n:n + nch])
        if n < w:
            u_ref[:, n:n + nch] = res.astype(BF16)
        else:
            gs_ref[:, n - w:n - w + nch] = _silu(res).astype(BF16)


def _lru_in(ctx, x, shift2, scale2, g, w_in):
    nb, c, d = ctx.shape
    s = x.shape[1]
    w = w_in.shape[1] // 2
    tt = 16
    nc, nl = c // tt, s // tt
    m = tt * nb
    nch = min(512, w)
    kern = functools.partial(_lru_in_kernel, nc=nc, nb=nb, tt=tt, d=d, w=w, nch=nch)
    out = jax.ShapeDtypeStruct(((c + s) * nb, w), BF16)
    return pl.pallas_call(
        kern,
        grid=(nc + nl,),
        in_specs=[
            pl.BlockSpec((nb, tt, d), lambda j: (0, jnp.minimum(j, nc - 1), 0)),
            pl.BlockSpec((nb, tt, d), lambda j: (0, jnp.maximum(j - nc, 0), 0)),
            pl.BlockSpec((1, nb, d), lambda j: (jnp.where(j < nc, 0, 1), 0, 0)),
            pl.BlockSpec((1, nb, d), lambda j: (jnp.where(j < nc, 0, 1), 0, 0)),
            pl.BlockSpec((1, d), lambda j: (0, 0)),
            _resident((d, 2 * w)),
        ],
        out_specs=[pl.BlockSpec((m, w), lambda j: (j, 0)), pl.BlockSpec((m, w), lambda j: (j, 0))],
        out_shape=[out, out],
        scratch_shapes=[pltpu.VMEM((d // LANES, m, LANES), F32)],
        compiler_params=_params(("parallel",)),
        name="lru_in",
    )(ctx, x, shift2, scale2, g, w_in)


def _scan_chunk(j, nc, ntot, reverse):
    if not reverse:
        return j
    return jnp.where(j < nc, nc - 1 - j, ntot - 1 - j + nc)


def _lru_scan_kernel(prev_ref, cur_ref, next_ref, cw_ref, cb_ref, gw_ref, gb_ref, lam_ref, hs_ref,
                     a_ref, b_ref, h_ref, *, nc, ntot, nb, tc, wb, reverse):
    j = pl.program_id(1)
    c = _scan_chunk(j, nc, ntot, reverse)
    seq_start = (c == 0) | (c == nc)
    seq_end = (c == nc - 1) | (c == ntot - 1)
    m = tc * nb

    @pl.when(j == 0)
    def _():
        h_ref[...] = jnp.zeros_like(h_ref)

    prev = jnp.where(seq_start, 0.0, prev_ref[...].astype(F32))
    nxt = jnp.where(seq_end, 0.0, next_ref[0:nb, :].astype(F32))
    ext = jnp.concatenate([prev, cur_ref[...].astype(F32), nxt], axis=0)
    cw = cw_ref[...]
    xc = cb_ref[...] + cw[0:1] * ext[0:m]
    for k in range(1, 4):
        xc = xc + cw[k:k + 1] * ext[k * nb:k * nb + m]

    lam = lam_ref[...]
    sp = jnp.maximum(-lam, 0.0) + jnp.log1p(jnp.exp(-jnp.abs(lam)))
    nslab = wb // LANES
    for s in range(nslab):
        xs = xc[:, s * LANES:(s + 1) * LANES]
        gates = _sigmoid(_dot(xs.astype(BF16), gw_ref[s]) + gb_ref[s])
        log_a = (-LRU_C) * gates[:, :LANES] * sp[:, s * LANES:(s + 1) * LANES]
        a = jnp.exp(log_a)
        a_ref[s] = a
        b_ref[s] = jnp.sqrt(-jnp.tanh(log_a) * (a * a + 1.0)) * (gates[:, LANES:] * xs)

    def step(i, hs):
        t = (tc - 1 - i) if reverse else i
        rows = pl.ds(pl.multiple_of(t * nb, nb), nb)
        out = []
        for s in range(nslab):
            h = a_ref[s, rows, :] * hs[s] + b_ref[s, rows, :]
            hs_ref[rows, s * LANES:(s + 1) * LANES] = h
            out.append(h)
        return tuple(out)

    hs = lax.fori_loop(0, tc, step, tuple(h_ref[s] for s in range(nslab)), unroll=4)
    for s in range(nslab):
        h_ref[s] = hs[s]


def _lru_scan(u_tm, conv_w, conv_b, gw, gb, lam, *, nb, c, reverse):
    rows, w = u_tm.shape
    ltot = rows // nb
    tc = 32
    wb = min(512, w)
    nc, ntot = c // tc, ltot // tc
    m = tc * nb
    hb = tc // 2
    last_halo = ltot // 2 - 1
    chunk = functools.partial(_scan_chunk, nc=nc, ntot=ntot, reverse=reverse)
    kern = functools.partial(_lru_scan_kernel, nc=nc, ntot=ntot, nb=nb, tc=tc, wb=wb, reverse=reverse)
    nslab = wb // LANES
    return pl.pallas_call(
        kern,
        grid=(w // wb, ntot),
        in_specs=[
            pl.BlockSpec((2 * nb, wb), lambda i, j: (jnp.maximum(chunk(j) * hb - 1, 0), i)),
            pl.BlockSpec((m, wb), lambda i, j: (chunk(j), i)),
            pl.BlockSpec((2 * nb, wb), lambda i, j: (jnp.minimum((chunk(j) + 1) * hb, last_halo), i)),
            pl.BlockSpec((4, wb), lambda i, j: (0, i)),
            pl.BlockSpec((1, wb), lambda i, j: (0, i)),
            pl.BlockSpec((nslab, LANES, 2 * LANES), lambda i, j: (i, 0, 0)),
            pl.BlockSpec((nslab, 1, 2 * LANES), lambda i, j: (i, 0, 0)),
            pl.BlockSpec((1, wb), lambda i, j: (0, i)),
        ],
        out_specs=pl.BlockSpec((m, wb), lambda i, j: (chunk(j), i)),
        out_shape=jax.ShapeDtypeStruct((rows, w), F32),
        scratch_shapes=[
            pltpu.VMEM((nslab, m, LANES), F32),
            pltpu.VMEM((nslab, m, LANES), F32),
            pltpu.VMEM((nslab, nb, LANES), F32),
        ],
        compiler_params=_params(("parallel", "arbitrary")),
        name="lru_scan_bwd" if reverse else "lru_scan_fwd",
    )(u_tm, u_tm, u_tm, conv_w, conv_b, gw, gb, lam)


def _lru_out_kernel(hf_ref, hb_ref, gs_ref, w_ref, x_ref, gate_ref, g_ref, sh_ref, sc_ref, x1_ref, h1_ref, slab_ref,
                    *, nb, tt, d):
    z = ((hf_ref[...] + hb_ref[...]) * gs_ref[...].astype(F32)).astype(BF16)
    y = _dot(z, w_ref[...])
    nslab = d // LANES
    for s in range(nslab):
        slab_ref[s] = y[:, s * LANES:(s + 1) * LANES]
    g = g_ref[...]
    for b in range(nb):
        yb = jnp.concatenate([slab_ref[s, pl.ds(b, tt, stride=nb), :] for s in range(nslab)], axis=1)
        x1 = x_ref[b] + gate_ref[b] * yb
        x1_ref[b] = x1
        h1_ref[b] = _norm_mod(x1, g, sc_ref[b], sh_ref[b]).astype(BF16)


def _lru_out(hs_f, hs_b, gs, w_out, xres, gate, g1, shift1, scale1, *, row_off):
    nb, l, d = xres.shape
    w = w_out.shape[0]
    tt = 16
    m = tt * nb
    off = row_off // tt
    kern = functools.partial(_lru_out_kernel, nb=nb, tt=tt, d=d)
    bmod = pl.BlockSpec((nb, 1, d), lambda j: (0, 0, 0))
    return pl.pallas_call(
        kern,
        grid=(l // tt,),
        in_specs=[
            pl.BlockSpec((m, w), lambda j: (j + off, 0)),
            pl.BlockSpec((m, w), lambda j: (j + off, 0)),
            pl.BlockSpec((m, w), lambda j: (j + off, 0)),
            _resident((w, d)),
            pl.BlockSpec((nb, tt, d), lambda j: (0, j, 0)),
            bmod,
            pl.BlockSpec((1, d), lambda j: (0, 0)),
            bmod,
            bmod,
        ],
        out_specs=[pl.BlockSpec((nb, tt, d), lambda j: (0, j, 0)), pl.BlockSpec((nb, tt, d), lambda j: (0, j, 0))],
        out_shape=[jax.ShapeDtypeStruct((nb, l, d), F32), jax.ShapeDtypeStruct((nb, l, d), BF16)],
        scratch_shapes=[pltpu.VMEM((d // LANES, m, LANES), F32)],
        compiler_params=_params(("parallel",)),
        name="lru_out",
    )(hs_f, hs_b, gs, w_out, xres, gate, g1, shift1, scale1)


def _att_proj_kernel(*refs, mode, rope, nheads, aliased):
    refs = list(refs)
    h_ref, w_ref = refs[0], refs[1]
    pos = 2
    if mode in ("q", "k"):
        gain_ref, gmat_ref = refs[pos], refs[pos + 1]
        pos += 2
        if rope:
            cos_ref, sa_ref, sb_ref = refs[pos:pos + 3]
            pos += 3
    if aliased:
        pos += 1
    outs = refs[pos:]
    res = _dot(h_ref[0], w_ref[...])
    if mode == "g":
        outs[0][0] = _silu(res).astype(BF16)
        return
    if mode == "v":
        for h in range(nheads):
            outs[0][0, h] = res[:, h * LANES:(h + 1) * LANES].astype(BF16)
        return
    lane = lax.broadcasted_iota(jnp.int32, (1, LANES), 1)
    for p in range(res.shape[1] // MXU_DIM):
        r = res[:, p * MXU_DIM:(p + 1) * MXU_DIM]
        ssq = _dot((r * r).astype(BF16), gmat_ref[...])
        rn = r * lax.rsqrt(ssq * (1.0 / HEAD_DIM) + EPS) * gain_ref[:, p * MXU_DIM:(p + 1) * MXU_DIM]
        for hh in range(MXU_DIM // LANES):
            xh = rn[:, hh * LANES:(hh + 1) * LANES]
            if rope:
                xh = (xh * cos_ref[...] + pltpu.roll(xh, LANES - ROPE_FREQS, 1) * sa_ref[...]
                      + pltpu.roll(xh, ROPE_FREQS, 1) * sb_ref[...])
            h = p * (MXU_DIM // LANES) + hh
            if mode == "q":
                outs[0][0, h] = jnp.where(lane < HEAD_DIM, xh, 0.0).astype(BF16)
                outs[1][0, h] = jnp.where(lane >= HEAD_DIM, xh, 0.0).astype(BF16)
            else:
                outs[0][0, h] = xh.astype(BF16)


def _att_proj(h1, w, *, mode, gain=None, gmat=None, rope_tabs=None, out_len=None, out_off=0, into=None):
    nb, l, d = h1.shape
    n = w.shape[1]
    tr = min(512, l)
    nheads = n // LANES
    out_len = l if out_len is None else out_len
    off = out_off // tr
    rope = rope_tabs is not None
    args = [h1, w]
    in_specs = [pl.BlockSpec((1, tr, d), lambda b, i: (b, i, 0)), _resident((d, n))]
    if mode in ("q", "k"):
        args += [gain, gmat]
        in_specs += [pl.BlockSpec((1, n), lambda b, i: (0, 0)), pl.BlockSpec((MXU_DIM, MXU_DIM), lambda b, i: (0, 0))]
        if rope:
            args += list(rope_tabs)
            in_specs += [pl.BlockSpec((tr, LANES), lambda b, i: (i, 0))] * 3
    aliases = {}
    if into is not None:
        aliases = {len(args): 0}
        args.append(into)
        in_specs.append(pl.BlockSpec(memory_space=pl.ANY))
    if mode == "g":
        out_specs = [pl.BlockSpec((1, tr, n), lambda b, i: (b, i, 0))]
        out_shape = [jax.ShapeDtypeStruct((nb, l, n), BF16)]
    else:
        hspec = pl.BlockSpec((1, nheads, tr, LANES), lambda b, i: (b, 0, i + off, 0))
        hshape = jax.ShapeDtypeStruct((nb, nheads, out_len, LANES), BF16)
        nout = 2 if mode == "q" else 1
        out_specs, out_shape = [hspec] * nout, [hshape] * nout
    kern = functools.partial(_att_proj_kernel, mode=mode, rope=rope, nheads=nheads, aliased=into is not None)
    return pl.pallas_call(
        kern,
        grid=(nb, l // tr),
        in_specs=in_specs,
        out_specs=out_specs,
        out_shape=out_shape,
        input_output_aliases=aliases,
        compiler_params=_params(("parallel", "parallel")),
        name="att_proj_" + mode + ("_rope" if rope else ""),
    )(*args)


def _attn_kernel(qa_ref, qb_ref, k_ref, v_ref, lv_ref, sg_ref, o_ref, *, lam_init):
    lv = lv_ref[...]
    lam = (jnp.exp(jnp.sum(lv[0:1] * lv[1:2], axis=-1, keepdims=True))
           - jnp.exp(jnp.sum(lv[2:3] * lv[3:4], axis=-1, keepdims=True)) + lam_init)
    k = k_ref[0, 0]
    nt = (((1,), (1,)), ((), ()))

    def probs(q):
        s = lax.dot_general(q, k, nt, preferred_element_type=F32)
        p = jnp.exp(s - jnp.max(s, axis=-1, keepdims=True))
        return p, 1.0 / jnp.sum(p, axis=-1, keepdims=True)

    p0, r0 = probs(qa_ref[0, 0])
    p1, r1 = probs(qb_ref[0, 0])
    attn = (p0 * r0 - p1 * (lam * r1)).astype(BF16)
    o = _dot(attn, v_ref[0, 0])
    ms = jnp.mean(o * o, axis=-1, keepdims=True)
    o_ref[0, 0] = (o * lax.rsqrt(ms + EPS) * (sg_ref[...] * (1.0 - lam_init))).astype(BF16)


def _attention(qa, qb, kh, vh, lam_vecs, subln, *, lam_init):
    nb, nh, s, _ = qa.shape
    ltot = kh.shape[2]
    tq = min(256, s)
    qspec = pl.BlockSpec((1, 1, tq, LANES), lambda b, h, i: (b, h, i, 0))
    kspec = pl.BlockSpec((1, 1, ltot, LANES), lambda b, h, i: (b, h, 0, 0))
    return pl.pallas_call(
        functools.partial(_attn_kernel, lam_init=lam_init),
        grid=(nb, nh, s // tq),
        in_specs=[qspec, qspec, kspec, kspec,
                  pl.BlockSpec((4, HEAD_DIM), lambda b, h, i: (0, 0)),
                  pl.BlockSpec((1, LANES), lambda b, h, i: (0, 0))],
        out_specs=qspec,
        out_shape=jax.ShapeDtypeStruct((nb, nh, s, LANES), BF16),
        compiler_params=_params(("parallel", "parallel", "parallel")),
        name="diff_attention",
    )(qa, qb, kh, vh, lam_vecs, subln)


def _att_out_kernel(o_ref, gs_ref, w_ref, x_ref, gate_ref, out_ref, *, nheads):
    o = jnp.concatenate([o_ref[0, h] for h in range(nheads)], axis=1)
    z = (o.astype(F32) * gs_ref[0].astype(F32)).astype(BF16)
    out_ref[0] = x_ref[0] + gate_ref[0] * _dot(z, w_ref[...])


def _att_out(o_heads, gs, w_out, x1, gate):
    nb, nh, s, _ = o_heads.shape
    d = w_out.shape[1]
    wd = w_out.shape[0]
    tr = min(512, s)
    return pl.pallas_call(
        functools.partial(_att_out_kernel, nheads=nh),
        grid=(nb, s // tr),
        in_specs=[
            pl.BlockSpec((1, nh, tr, LANES), lambda b, i: (b, 0, i, 0)),
            pl.BlockSpec((1, tr, wd), lambda b, i: (b, i, 0)),
            _resident((wd, d)),
            pl.BlockSpec((1, tr, d), lambda b, i: (b, i, 0)),
            pl.BlockSpec((1, 1, d), lambda b, i: (b, 0, 0)),
        ],
        out_specs=pl.BlockSpec((1, tr, d), lambda b, i: (b, i, 0)),
        out_shape=jax.ShapeDtypeStruct((nb, s, d), F32),
        compiler_params=_params(("parallel", "parallel")),
        name="att_out",
    )(o_heads, gs, w_out, x1, gate)


def _rope_tables(s):
    t = jnp.arange(s)
    pos = jnp.stack([(t // GRID_W).astype(F32), (t % GRID_W).astype(F32)], axis=1)
    inv = ROPE_BASE ** (-jnp.arange(ROPE_FREQS, dtype=F32) / ROPE_FREQS)
    ang = pos[:, :, None] * inv
    lane = jnp.arange(LANES)
    dd = lane % HEAD_DIM
    axis, half, f = dd // (2 * ROPE_FREQS), (dd % (2 * ROPE_FREQS)) // ROPE_FREQS, dd % ROPE_FREQS
    cos = jnp.cos(ang)[:, axis, f]
    sin = jnp.sin(ang)[:, axis, f]
    return cos, jnp.where(half == 0, -sin, 0.0), jnp.where(half == 1, sin, 0.0)


def kernel(x, c, ctx, c_ctx, mod_w, mod_b, norm_g, lru_w_in, lru_conv_w, lru_conv_b, lru_gate_w, lru_gate_b,
           lru_lambda, lru_w_out, att_w_in, att_q_norm, att_k_norm, att_lambda, att_subln, att_w_out):
    nb, s, d = x.shape
    clen = ctx.shape[1]
    w = lru_w_in.shape[-1] // 2
    aw = att_w_out.shape[1]
    nheads = aw // LANES
    assert lru_gate_w.shape[-1] == LANES and d % MXU_DIM == 0 and nb % 8 == 0

    rows = ((nb + 1 + 7) // 8) * 8
    cs = jnp.zeros((rows, d), F32).at[:nb].set(c).at[nb].set(c_ctx)
    mod = _modulation(cs, mod_w, mod_b)

    def split(i):
        lat = [mod[i, :nb, k * d:(k + 1) * d] for k in range(3)]
        cx = [jnp.broadcast_to(mod[i, nb:nb + 1, k * d:(k + 1) * d], (nb, d)) for k in range(3)]
        return lat, cx

    (shift0, scale0, gate0), (cshift0, cscale0, cgate0) = split(0)
    (shift1, scale1, gate1), (cshift1, cscale1, _) = split(1)
    col = lambda a: a.reshape(nb, 1, d)

    u_tm, gs_tm = _lru_in(ctx, x, jnp.stack([cshift0, shift0]), jnp.stack([cscale0, scale0]),
                          norm_g[0:1], lru_w_in[0].astype(BF16))
    hs = []
    for dr in range(2):
        gw = jnp.concatenate([lru_gate_w[0, dr, 0], lru_gate_w[0, dr, 1]], axis=-1).astype(BF16)
        gb = jnp.concatenate([lru_gate_b[0, dr, 0], lru_gate_b[0, dr, 1]], axis=-1)[:, None, :]
        hs.append(_lru_scan(u_tm, lru_conv_w[0], lru_conv_b[0:1], gw, gb, lru_lambda[0, dr:dr + 1],
                            nb=nb, c=clen, reverse=bool(dr)))
    w_out0 = lru_w_out[0].astype(BF16)
    g1 = norm_g[1:2]
    _, h1_ctx = _lru_out(hs[0], hs[1], gs_tm, w_out0, ctx, col(cgate0), g1, col(cshift1), col(cscale1), row_off=0)
    x1, h1_lat = _lru_out(hs[0], hs[1], gs_tm, w_out0, x, col(gate0), g1, col(shift1), col(scale1), row_off=clen)

    lam_init = 0.8 - 0.6 * math.exp(-0.3 * 1)
    wq, wk, wv, wg = (att_w_in[0][:, k * aw:(k + 1) * aw].astype(BF16) for k in range(4))
    gmat = jnp.kron(jnp.eye(MXU_DIM // HEAD_DIM, dtype=F32), jnp.ones((HEAD_DIM, HEAD_DIM), F32)).astype(BF16)
    reps = aw // HEAD_DIM
    q_gain = jnp.tile(att_q_norm[0] * (HEAD_DIM ** -0.5), reps)[None, :]
    k_gain = jnp.tile(att_k_norm[0], reps)[None, :]
    tabs = _rope_tables(s)
    ltot = s + clen
    qa, qb = _att_proj(h1_lat, wq, mode="q", gain=q_gain, gmat=gmat, rope_tabs=tabs)
    kh, = _att_proj(h1_lat, wk, mode="k", gain=k_gain, gmat=gmat, rope_tabs=tabs, out_len=ltot)
    kh, = _att_proj(h1_ctx, wk, mode="k", gain=k_gain, gmat=gmat, out_len=ltot, out_off=s, into=kh)
    vh, = _att_proj(h1_lat, wv, mode="v", out_len=ltot)
    vh, = _att_proj(h1_ctx, wv, mode="v", out_len=ltot, out_off=s, into=vh)
    gsa, = _att_proj(h1_lat, wg, mode="g")
    o_heads = _attention(qa, qb, kh, vh, att_lambda[0], att_subln[0:1], lam_init=lam_init)
    return _att_out(o_heads, gsa, att_w_out[0].astype(BF16), x1, col(gate1))
```

```python
import functools
import math

import jax
import jax.numpy as jnp
from jax import lax
from jax.experimental import pallas as pl
from jax.experimental.pallas import tpu as pltpu

F32 = jnp.float32
BF16 = jnp.bfloat16

EPS = 1e-6
LRU_C = 8.0
GRID_W = 64
HEAD_DIM = 64
ROPE_FREQS = HEAD_DIM // 4
ROPE_BASE = 10000.0
LANES = 128
MXU_DIM = 256
VMEM_LIMIT = 56 * 1024 * 1024
V_ROWS = LANES + 16
ATT_TQ = 256
_NT = (((1,), (1,)), ((), ()))


def _params(sem, vmem=VMEM_LIMIT):
    return pltpu.CompilerParams(dimension_semantics=sem, vmem_limit_bytes=vmem)


def _sigmoid(x):
    return 1.0 / (1.0 + jnp.exp(-x))


def _silu(x):
    return x * _sigmoid(x)


def _dot(a, b):
    return jnp.dot(a, b, preferred_element_type=F32)


def _resident(shape):
    nd = len(shape)
    return pl.BlockSpec(shape, lambda *_: (0,) * nd, pipeline_mode=pl.Buffered(1))


def _mod_kernel(cs_ref, w_ref, b_ref, o_ref):
    a = _silu(cs_ref[...])
    w = w_ref[0]
    a_hi = a.astype(BF16)
    a_lo = (a - a_hi.astype(F32)).astype(BF16)
    w_hi = w.astype(BF16)
    w_lo = (w - w_hi.astype(F32)).astype(BF16)
    acc = _dot(a_hi, w_hi) + _dot(a_lo, w_hi) + _dot(a_hi, w_lo)
    o_ref[0] = acc + b_ref[0]


def _modulation(cs, mod_w, mod_b):
    depth, d, n3 = mod_w.shape
    rows = cs.shape[0]
    tn = 768 if n3 % 768 == 0 else n3
    return pl.pallas_call(
        _mod_kernel,
        grid=(depth, n3 // tn),
        in_specs=[
            pl.BlockSpec((rows, d), lambda i, n: (0, 0)),
            pl.BlockSpec((1, d, tn), lambda i, n: (i, 0, n)),
            pl.BlockSpec((1, 1, tn), lambda i, n: (i, 0, n)),
        ],
        out_specs=pl.BlockSpec((1, rows, tn), lambda i, n: (i, 0, n)),
        out_shape=jax.ShapeDtypeStruct((depth, rows, n3), F32),
        compiler_params=_params(("parallel", "parallel")),
        name="modulation",
    )(cs, mod_w, mod_b.reshape(depth, 1, n3))


def _norm_mod(xb, g, scale, shift):
    ms = jnp.mean(xb * xb, axis=-1, keepdims=True)
    return (xb * lax.rsqrt(ms + EPS) * g) * (1.0 + scale) + shift


def _lru_in_kernel(ctx_ref, x_ref, sh_ref, sc_ref, g_ref, w_ref, u_ref, gs_ref, slab_ref, *, nc, nb, tt, d, w, nch):
    is_ctx = pl.program_id(0) < nc
    g = g_ref[...]
    for b in range(nb):
        xb = jnp.where(is_ctx, ctx_ref[b], x_ref[b])
        h = _norm_mod(xb, g, sc_ref[0, b:b + 1, :], sh_ref[0, b:b + 1, :])
        for s in range(d // LANES):
            slab_ref[s, pl.ds(b, tt, stride=nb), :] = h[:, s * LANES:(s + 1) * LANES]
    hfull = jnp.concatenate([slab_ref[s] for s in range(d // LANES)], axis=1).astype(BF16)
    for n in range(0, 2 * w, nch):
        res = _dot(hfull, w_ref[:, n:n + nch])
        if n < w:
            u_ref[:, n:n + nch] = res.astype(BF16)
        else:
            gs_ref[:, n - w:n - w + nch] = _silu(res).astype(BF16)


def _lru_in(ctx, x, shift2, scale2, g, w_in):
    nb, c, d = ctx.shape
    s = x.shape[1]
    w = w_in.shape[1] // 2
    tt = 16
    nc, nl = c // tt, s // tt
    m = tt * nb
    nch = min(512, w)
    kern = functools.partial(_lru_in_kernel, nc=nc, nb=nb, tt=tt, d=d, w=w, nch=nch)
    out = jax.ShapeDtypeStruct(((c + s) * nb, w), BF16)
    return pl.pallas_call(
        kern,
        grid=(nc + nl,),
        in_specs=[
            pl.BlockSpec((nb, tt, d), lambda j: (0, jnp.minimum(j, nc - 1), 0)),
            pl.BlockSpec((nb, tt, d), lambda j: (0, jnp.maximum(j - nc, 0), 0)),
            pl.BlockSpec((1, nb, d), lambda j: (jnp.where(j < nc, 0, 1), 0, 0)),
            pl.BlockSpec((1, nb, d), lambda j: (jnp.where(j < nc, 0, 1), 0, 0)),
            pl.BlockSpec((1, d), lambda j: (0, 0)),
            _resident((d, 2 * w)),
        ],
        out_specs=[pl.BlockSpec((m, w), lambda j: (j, 0)), pl.BlockSpec((m, w), lambda j: (j, 0))],
        out_shape=[out, out],
        scratch_shapes=[pltpu.VMEM((d // LANES, m, LANES), F32)],
        compiler_params=_params(("parallel",)),
        name="lru_in",
    )(ctx, x, shift2, scale2, g, w_in)


def _scan_chunk(j, nc, ntot, reverse):
    if not reverse:
        return j
    return jnp.where(j < nc, nc - 1 - j, ntot - 1 - j + nc)


def _lru_scan_kernel(prev_ref, cur_ref, next_ref, cw_ref, cb_ref, gw_ref, gb_ref, lam_ref, hs_ref,
                     a_ref, b_ref, h_ref, *, nc, ntot, nb, tc, wb, reverse):
    j = pl.program_id(1)
    c = _scan_chunk(j, nc, ntot, reverse)
    seq_start = (c == 0) | (c == nc)
    seq_end = (c == nc - 1) | (c == ntot - 1)
    m = tc * nb

    @pl.when(j == 0)
    def _():
        h_ref[...] = jnp.zeros_like(h_ref)

    prev = jnp.where(seq_start, 0.0, prev_ref[...].astype(F32))
    nxt = jnp.where(seq_end, 0.0, next_ref[0:nb, :].astype(F32))
    ext = jnp.concatenate([prev, cur_ref[...].astype(F32), nxt], axis=0)
    cw = cw_ref[...]
    xc = cb_ref[...] + cw[0:1] * ext[0:m]
    for k in range(1, 4):
        xc = xc + cw[k:k + 1] * ext[k * nb:k * nb + m]

    lam = lam_ref[...]
    sp = jnp.maximum(-lam, 0.0) + jnp.log1p(jnp.exp(-jnp.abs(lam)))
    nslab = wb // LANES
    for s in range(nslab):
        xs = xc[:, s * LANES:(s + 1) * LANES]
        gates = _sigmoid(_dot(xs.astype(BF16), gw_ref[s]) + gb_ref[s])
        log_a = (-LRU_C) * gates[:, :LANES] * sp[:, s * LANES:(s + 1) * LANES]
        a = jnp.exp(log_a)
        a_ref[s] = a
        b_ref[s] = jnp.sqrt(-jnp.tanh(log_a) * (a * a + 1.0)) * (gates[:, LANES:] * xs)

    def step(i, hs):
        t = (tc - 1 - i) if reverse else i
        rows = pl.ds(pl.multiple_of(t * nb, nb), nb)
        out = []
        for s in range(nslab):
            h = a_ref[s, rows, :] * hs[s] + b_ref[s, rows, :]
            hs_ref[rows, s * LANES:(s + 1) * LANES] = h
            out.append(h)
        return tuple(out)

    hs = lax.fori_loop(0, tc, step, tuple(h_ref[s] for s in range(nslab)), unroll=4)
    for s in range(nslab):
        h_ref[s] = hs[s]


def _lru_scan(u_tm, conv_w, conv_b, gw, gb, lam, *, nb, c, reverse):
    rows, w = u_tm.shape
    ltot = rows // nb
    tc = 32
    wb = min(512, w)
    nc, ntot = c // tc, ltot // tc
    m = tc * nb
    hb = tc // 2
    last_halo = ltot // 2 - 1
    chunk = functools.partial(_scan_chunk, nc=nc, ntot=ntot, reverse=reverse)
    kern = functools.partial(_lru_scan_kernel, nc=nc, ntot=ntot, nb=nb, tc=tc, wb=wb, reverse=reverse)
    nslab = wb // LANES
    return pl.pallas_call(
        kern,
        grid=(w // wb, ntot),
        in_specs=[
            pl.BlockSpec((2 * nb, wb), lambda i, j: (jnp.maximum(chunk(j) * hb - 1, 0), i)),
            pl.BlockSpec((m, wb), lambda i, j: (chunk(j), i)),
            pl.BlockSpec((2 * nb, wb), lambda i, j: (jnp.minimum((chunk(j) + 1) * hb, last_halo), i)),
            pl.BlockSpec((4, wb), lambda i, j: (0, i)),
            pl.BlockSpec((1, wb), lambda i, j: (0, i)),
            pl.BlockSpec((nslab, LANES, 2 * LANES), lambda i, j: (i, 0, 0)),
            pl.BlockSpec((nslab, 1, 2 * LANES), lambda i, j: (i, 0, 0)),
            pl.BlockSpec((1, wb), lambda i, j: (0, i)),
        ],
        out_specs=pl.BlockSpec((m, wb), lambda i, j: (chunk(j), i)),
        out_shape=jax.ShapeDtypeStruct((rows, w), F32),
        scratch_shapes=[
            pltpu.VMEM((nslab, m, LANES), F32),
            pltpu.VMEM((nslab, m, LANES), F32),
            pltpu.VMEM((nslab, nb, LANES), F32),
        ],
        compiler_params=_params(("parallel", "arbitrary")),
        name="lru_scan_bwd" if reverse else "lru_scan_fwd",
    )(u_tm, u_tm, u_tm, conv_w, conv_b, gw, gb, lam)


def _lru_out_kernel(hf_ref, hb_ref, gs_ref, w_ref, x_ref, gate_ref, g_ref, sh_ref, sc_ref, x1_ref, h1_ref, slab_ref,
                    *, nb, tt, d):
    z = ((hf_ref[...] + hb_ref[...]) * gs_ref[...].astype(F32)).astype(BF16)
    y = _dot(z, w_ref[...])
    nslab = d // LANES
    for s in range(nslab):
        slab_ref[s] = y[:, s * LANES:(s + 1) * LANES]
    g = g_ref[...]
    for b in range(nb):
        yb = jnp.concatenate([slab_ref[s, pl.ds(b, tt, stride=nb), :] for s in range(nslab)], axis=1)
        x1 = x_ref[b] + gate_ref[b] * yb
        x1_ref[b] = x1
        h1_ref[b] = _norm_mod(x1, g, sc_ref[b], sh_ref[b]).astype(BF16)


def _lru_out(hs_f, hs_b, gs, w_out, xres, gate, g1, shift1, scale1, *, row_off):
    nb, l, d = xres.shape
    w = w_out.shape[0]
    tt = 16
    m = tt * nb
    off = row_off // tt
    kern = functools.partial(_lru_out_kernel, nb=nb, tt=tt, d=d)
    bmod = pl.BlockSpec((nb, 1, d), lambda j: (0, 0, 0))
    return pl.pallas_call(
        kern,
        grid=(l // tt,),
        in_specs=[
            pl.BlockSpec((m, w), lambda j: (j + off, 0)),
            pl.BlockSpec((m, w), lambda j: (j + off, 0)),
            pl.BlockSpec((m, w), lambda j: (j + off, 0)),
            _resident((w, d)),
            pl.BlockSpec((nb, tt, d), lambda j: (0, j, 0)),
            bmod,
            pl.BlockSpec((1, d), lambda j: (0, 0)),
            bmod,
            bmod,
        ],
        out_specs=[pl.BlockSpec((nb, tt, d), lambda j: (0, j, 0)), pl.BlockSpec((nb, tt, d), lambda j: (0, j, 0))],
        out_shape=[jax.ShapeDtypeStruct((nb, l, d), F32), jax.ShapeDtypeStruct((nb, l, d), BF16)],
        scratch_shapes=[pltpu.VMEM((d // LANES, m, LANES), F32)],
        compiler_params=_params(("parallel",)),
        name="lru_out",
    )(hs_f, hs_b, gs, w_out, xres, gate, g1, shift1, scale1)


def _att_proj_kernel(*refs, mode, rope, nheads, aliased):
    refs = list(refs)
    h_ref, w_ref = refs[0], refs[1]
    pos = 2
    if mode in ("q", "k"):
        gain_ref, gmat_ref = refs[pos], refs[pos + 1]
        pos += 2
        if rope:
            cos_ref, sa_ref, sb_ref = refs[pos:pos + 3]
            pos += 3
    if aliased:
        pos += 1
    outs = refs[pos:]
    if mode == "v":
        res_t = lax.dot_general(w_ref[...], h_ref[0], _NT, preferred_element_type=F32)
        ones = jnp.ones((V_ROWS - LANES, res_t.shape[1]), BF16)
        for h in range(nheads):
            outs[0][0, h, 0:LANES, :] = res_t[h * LANES:(h + 1) * LANES, :].astype(BF16)
            outs[0][0, h, LANES:V_ROWS, :] = ones
        return
    res = _dot(h_ref[0], w_ref[...])
    if mode == "g":
        outs[0][0] = _silu(res).astype(BF16)
        return
    lane = lax.broadcasted_iota(jnp.int32, (1, LANES), 1)
    for p in range(res.shape[1] // MXU_DIM):
        r = res[:, p * MXU_DIM:(p + 1) * MXU_DIM]
        ssq = _dot((r * r).astype(BF16), gmat_ref[...])
        rn = r * lax.rsqrt(ssq * (1.0 / HEAD_DIM) + EPS) * gain_ref[:, p * MXU_DIM:(p + 1) * MXU_DIM]
        for hh in range(MXU_DIM // LANES):
            xh = rn[:, hh * LANES:(hh + 1) * LANES]
            if rope:
                xh = (xh * cos_ref[...] + pltpu.roll(xh, LANES - ROPE_FREQS, 1) * sa_ref[...]
                      + pltpu.roll(xh, ROPE_FREQS, 1) * sb_ref[...])
            h = p * (MXU_DIM // LANES) + hh
            if mode == "q":
                outs[0][0, h] = jnp.where(lane < HEAD_DIM, xh, 0.0).astype(BF16)
                outs[1][0, h] = jnp.where(lane >= HEAD_DIM, xh, 0.0).astype(BF16)
            else:
                outs[0][0, h] = xh.astype(BF16)


def _att_proj(h1, w, *, mode, gain=None, gmat=None, rope_tabs=None, out_len=None, out_off=0, into=None):
    nb, l, d = h1.shape
    n = w.shape[0] if mode == "v" else w.shape[1]
    tr = min(512, l)
    nheads = n // LANES
    out_len = l if out_len is None else out_len
    off = out_off // tr
    rope = rope_tabs is not None
    args = [h1, w]
    in_specs = [pl.BlockSpec((1, tr, d), lambda b, i: (b, i, 0)), _resident(w.shape)]
    if mode in ("q", "k"):
        args += [gain, gmat]
        in_specs += [pl.BlockSpec((1, n), lambda b, i: (0, 0)), pl.BlockSpec((MXU_DIM, MXU_DIM), lambda b, i: (0, 0))]
        if rope:
            args += list(rope_tabs)
            in_specs += [pl.BlockSpec((tr, LANES), lambda b, i: (i, 0))] * 3
    aliases = {}
    if into is not None:
        aliases = {len(args): 0}
        args.append(into)
        in_specs.append(pl.BlockSpec(memory_space=pl.ANY))
    if mode == "g":
        out_specs = [pl.BlockSpec((1, tr, n), lambda b, i: (b, i, 0))]
        out_shape = [jax.ShapeDtypeStruct((nb, l, n), BF16)]
    elif mode == "v":
        out_specs = [pl.BlockSpec((1, nheads, V_ROWS, tr), lambda b, i: (b, 0, 0, i + off))]
        out_shape = [jax.ShapeDtypeStruct((nb, nheads, V_ROWS, out_len), BF16)]
    else:
        hspec = pl.BlockSpec((1, nheads, tr, LANES), lambda b, i: (b, 0, i + off, 0))
        hshape = jax.ShapeDtypeStruct((nb, nheads, out_len, LANES), BF16)
        nout = 2 if mode == "q" else 1
        out_specs, out_shape = [hspec] * nout, [hshape] * nout
    kern = functools.partial(_att_proj_kernel, mode=mode, rope=rope, nheads=nheads, aliased=into is not None)
    return pl.pallas_call(
        kern,
        grid=(nb, l // tr),
        in_specs=in_specs,
        out_specs=out_specs,
        out_shape=out_shape,
        input_output_aliases=aliases,
        compiler_params=_params(("parallel", "parallel")),
        name="att_proj_" + mode + ("_rope" if rope else ""),
    )(*args)


def _attn_kernel(qa_ref, qb_ref, k_ref, vt_ref, lv_ref, sg_ref, o_ref, st_ref, m_ref, p_ref, *, lam_init, tq):
    @pl.when(pl.program_id(0) == 0)
    def _():
        st_ref[...] = jnp.zeros_like(st_ref)
        m_ref[...] = jnp.zeros_like(m_ref)
        p_ref[...] = jnp.ones_like(p_ref)

    lv = lv_ref[...]
    lam = (jnp.exp(jnp.sum(lv[0:1] * lv[1:2], axis=-1, keepdims=True))
           - jnp.exp(jnp.sum(lv[2:3] * lv[3:4], axis=-1, keepdims=True)) + lam_init)
    ot = _dot(vt_ref[0, 0], p_ref[...])
    on = ot[0:LANES] * (1.0 / ot[LANES:LANES + 1])
    od = on[:, 0:tq] - lam * on[:, tq:2 * tq]
    ms = jnp.mean(od * od, axis=0, keepdims=True)
    o_ref[0, 0] = (od * lax.rsqrt(ms + EPS) * sg_ref[...]).T.astype(BF16)

    p_ref[...] = jnp.exp2((st_ref[...] - m_ref[...]).astype(BF16))

    q = jnp.concatenate([qa_ref[0, 0], qb_ref[0, 0]], axis=0)
    st = lax.dot_general(k_ref[0, 0], q, _NT, preferred_element_type=F32)
    st_ref[...] = st
    m_ref[...] = jnp.max(st, axis=0, keepdims=True)


def _attention(qa, qb, kh, vt, lam_vecs, subln_t, *, lam_init):
    nb, nh, s, _ = qa.shape
    ltot = kh.shape[2]
    tq = subln_t.shape[1]
    nq = s // tq
    ntiles = nb * nh * nq

    def tile(t):
        return t // (nh * nq), (t // nq) % nh, t % nq

    def q_map(i):
        b, h, qi = tile(jnp.minimum(i, ntiles - 1))
        return b, h, qi, 0

    def k_map(i):
        b, h, _ = tile(jnp.minimum(i, ntiles - 1))
        return b, h, 0, 0

    def v_map(i):
        b, h, _ = tile(jnp.maximum(i - 2, 0))
        return b, h, 0, 0

    def o_map(i):
        b, h, qi = tile(jnp.maximum(i - 2, 0))
        return b, h, qi, 0

    qspec = pl.BlockSpec((1, 1, tq, LANES), q_map)
    return pl.pallas_call(
        functools.partial(_attn_kernel, lam_init=lam_init, tq=tq),
        grid=(ntiles + 2,),
        in_specs=[qspec, qspec,
                  pl.BlockSpec((1, 1, ltot, LANES), k_map),
                  pl.BlockSpec((1, 1, V_ROWS, ltot), v_map),
                  pl.BlockSpec((4, HEAD_DIM), lambda i: (0, 0)),
                  pl.BlockSpec((LANES, tq), lambda i: (0, 0))],
        out_specs=pl.BlockSpec((1, 1, tq, LANES), o_map),
        out_shape=jax.ShapeDtypeStruct((nb, nh, s, LANES), BF16),
        scratch_shapes=[pltpu.VMEM((ltot, 2 * tq), F32), pltpu.VMEM((1, 2 * tq), F32),
                        pltpu.VMEM((ltot, 2 * tq), BF16)],
        compiler_params=_params(("arbitrary",)),
        name="diff_attention",
    )(qa, qb, kh, vt, lam_vecs, subln_t)


def _att_out_kernel(o_ref, gs_ref, w_ref, x_ref, gate_ref, out_ref, *, nheads):
    o = jnp.concatenate([o_ref[0, h] for h in range(nheads)], axis=1)
    z = (o.astype(F32) * gs_ref[0].astype(F32)).astype(BF16)
    out_ref[0] = x_ref[0] + gate_ref[0] * _dot(z, w_ref[...])


def _att_out(o_heads, gs, w_out, x1, gate):
    nb, nh, s, _ = o_heads.shape
    d = w_out.shape[1]
    wd = w_out.shape[0]
    tr = min(512, s)
    return pl.pallas_call(
        functools.partial(_att_out_kernel, nheads=nh),
        grid=(nb, s // tr),
        in_specs=[
            pl.BlockSpec((1, nh, tr, LANES), lambda b, i: (b, 0, i, 0)),
            pl.BlockSpec((1, tr, wd), lambda b, i: (b, i, 0)),
            _resident((wd, d)),
            pl.BlockSpec((1, tr, d), lambda b, i: (b, i, 0)),
            pl.BlockSpec((1, 1, d), lambda b, i: (b, 0, 0)),
        ],
        out_specs=pl.BlockSpec((1, tr, d), lambda b, i: (b, i, 0)),
        out_shape=jax.ShapeDtypeStruct((nb, s, d), F32),
        compiler_params=_params(("parallel", "parallel")),
        name="att_out",
    )(o_heads, gs, w_out, x1, gate)


def _rope_tables(s):
    t = jnp.arange(s)
    pos = jnp.stack([(t // GRID_W).astype(F32), (t % GRID_W).astype(F32)], axis=1)
    inv = ROPE_BASE ** (-jnp.arange(ROPE_FREQS, dtype=F32) / ROPE_FREQS)
    ang = pos[:, :, None] * inv
    lane = jnp.arange(LANES)
    dd = lane % HEAD_DIM
    axis, half, f = dd // (2 * ROPE_FREQS), (dd % (2 * ROPE_FREQS)) // ROPE_FREQS, dd % ROPE_FREQS
    cos = jnp.cos(ang)[:, axis, f]
    sin = jnp.sin(ang)[:, axis, f]
    return cos, jnp.where(half == 0, -sin, 0.0), jnp.where(half == 1, sin, 0.0)


def kernel(x, c, ctx, c_ctx, mod_w, mod_b, norm_g, lru_w_in, lru_conv_w, lru_conv_b, lru_gate_w, lru_gate_b,
           lru_lambda, lru_w_out, att_w_in, att_q_norm, att_k_norm, att_lambda, att_subln, att_w_out):
    nb, s, d = x.shape
    clen = ctx.shape[1]
    w = lru_w_in.shape[-1] // 2
    aw = att_w_out.shape[1]
    nheads = aw // LANES
    assert lru_gate_w.shape[-1] == LANES and d % MXU_DIM == 0 and nb % 8 == 0

    rows = ((nb + 1 + 7) // 8) * 8
    cs = jnp.zeros((rows, d), F32).at[:nb].set(c).at[nb].set(c_ctx)
    mod = _modulation(cs, mod_w, mod_b)

    def split(i):
        lat = [mod[i, :nb, k * d:(k + 1) * d] for k in range(3)]
        cx = [jnp.broadcast_to(mod[i, nb:nb + 1, k * d:(k + 1) * d], (nb, d)) for k in range(3)]
        return lat, cx

    (shift0, scale0, gate0), (cshift0, cscale0, cgate0) = split(0)
    (shift1, scale1, gate1), (cshift1, cscale1, _) = split(1)
    col = lambda a: a.reshape(nb, 1, d)

    u_tm, gs_tm = _lru_in(ctx, x, jnp.stack([cshift0, shift0]), jnp.stack([cscale0, scale0]),
                          norm_g[0:1], lru_w_in[0].astype(BF16))
    hs = []
    for dr in range(2):
        gw = jnp.concatenate([lru_gate_w[0, dr, 0], lru_gate_w[0, dr, 1]], axis=-1).astype(BF16)
        gb = jnp.concatenate([lru_gate_b[0, dr, 0], lru_gate_b[0, dr, 1]], axis=-1)[:, None, :]
        hs.append(_lru_scan(u_tm, lru_conv_w[0], lru_conv_b[0:1], gw, gb, lru_lambda[0, dr:dr + 1],
                            nb=nb, c=clen, reverse=bool(dr)))
    w_out0 = lru_w_out[0].astype(BF16)
    g1 = norm_g[1:2]
    _, h1_ctx = _lru_out(hs[0], hs[1], gs_tm, w_out0, ctx, col(cgate0), g1, col(cshift1), col(cscale1), row_off=0)
    x1, h1_lat = _lru_out(hs[0], hs[1], gs_tm, w_out0, x, col(gate0), g1, col(shift1), col(scale1), row_off=clen)

    lam_init = 0.8 - 0.6 * math.exp(-0.3 * 1)
    wq, wk, wv, wg = (att_w_in[0][:, k * aw:(k + 1) * aw].astype(BF16) for k in range(4))
    gmat = jnp.kron(jnp.eye(MXU_DIM // HEAD_DIM, dtype=F32), jnp.ones((HEAD_DIM, HEAD_DIM), F32)).astype(BF16)
    reps = aw // HEAD_DIM
    q_gain = jnp.tile(att_q_norm[0] * (HEAD_DIM ** -0.5 * math.log2(math.e)), reps)[None, :]
    k_gain = jnp.tile(att_k_norm[0], reps)[None, :]
    tabs = _rope_tables(s)
    ltot = s + clen
    qa, qb = _att_proj(h1_lat, wq, mode="q", gain=q_gain, gmat=gmat, rope_tabs=tabs)
    kh, = _att_proj(h1_lat, wk, mode="k", gain=k_gain, gmat=gmat, rope_tabs=tabs, out_len=ltot)
    kh, = _att_proj(h1_ctx, wk, mode="k", gain=k_gain, gmat=gmat, out_len=ltot, out_off=s, into=kh)
    vt, = _att_proj(h1_lat, wv.T, mode="v", out_len=ltot)
    vt, = _att_proj(h1_ctx, wv.T, mode="v", out_len=ltot, out_off=s, into=vt)
    gsa, = _att_proj(h1_lat, wg, mode="g")
    subln_t = jnp.broadcast_to((att_subln[0] * (1.0 - lam_init))[:, None], (LANES, min(ATT_TQ, s)))
    o_heads = _attention(qa, qb, kh, vt, att_lambda[0], subln_t, lam_init=lam_init)
    return _att_out(o_heads, gsa, att_w_out[0].astype(BF16), x1, col(gate1))
```

```python
import functools
import math

import jax
import jax.numpy as jnp
from jax import lax
from jax.experimental import pallas as pl
from jax.experimental.pallas import tpu as pltpu

F32 = jnp.float32
BF16 = jnp.bfloat16

EPS = 1e-6
LRU_C = 8.0
GRID_W = 64
HEAD_DIM = 64
ROPE_FREQS = HEAD_DIM // 4
ROPE_BASE = 10000.0
LANES = 128
MXU_DIM = 256
VMEM_LIMIT = 56 * 1024 * 1024
V_ROWS = LANES + 16
ATT_TQ = 256
_NT = (((1,), (1,)), ((), ()))


def _params(sem, vmem=VMEM_LIMIT):
    return pltpu.CompilerParams(dimension_semantics=sem, vmem_limit_bytes=vmem)


def _sigmoid(x):
    return 1.0 / (1.0 + jnp.exp(-x))


def _silu(x):
    return x * _sigmoid(x)


def _dot(a, b):
    return jnp.dot(a, b, preferred_element_type=F32)


def _resident(shape):
    nd = len(shape)
    return pl.BlockSpec(shape, lambda *_: (0,) * nd, pipeline_mode=pl.Buffered(1))


def _mod_kernel(cs_ref, w_ref, b_ref, o_ref):
    a = _silu(cs_ref[...])
    w = w_ref[0]
    a_hi = a.astype(BF16)
    a_lo = (a - a_hi.astype(F32)).astype(BF16)
    w_hi = w.astype(BF16)
    w_lo = (w - w_hi.astype(F32)).astype(BF16)
    acc = _dot(a_hi, w_hi) + _dot(a_lo, w_hi) + _dot(a_hi, w_lo)
    o_ref[0] = acc + b_ref[0]


def _modulation(cs, mod_w, mod_b):
    depth, d, n3 = mod_w.shape
    rows = cs.shape[0]
    tn = 768 if n3 % 768 == 0 else n3
    return pl.pallas_call(
        _mod_kernel,
        grid=(depth, n3 // tn),
        in_specs=[
            pl.BlockSpec((rows, d), lambda i, n: (0, 0)),
            pl.BlockSpec((1, d, tn), lambda i, n: (i, 0, n)),
            pl.BlockSpec((1, 1, tn), lambda i, n: (i, 0, n)),
        ],
        out_specs=pl.BlockSpec((1, rows, tn), lambda i, n: (i, 0, n)),
        out_shape=jax.ShapeDtypeStruct((depth, rows, n3), F32),
        compiler_params=_params(("parallel", "parallel")),
        name="modulation",
    )(cs, mod_w, mod_b.reshape(depth, 1, n3))


def _norm_mod(xb, g, scale, shift):
    ms = jnp.mean(xb * xb, axis=-1, keepdims=True)
    return (xb * lax.rsqrt(ms + EPS) * g) * (1.0 + scale) + shift


def _lru_in_kernel(ctx_ref, x_ref, sh_ref, sc_ref, g_ref, w_ref, u_ref, gs_ref, slab_ref, *, nc, nb, tt, d, w, nch):
    is_ctx = pl.program_id(0) < nc
    g = g_ref[...]
    for b in range(nb):
        xb = jnp.where(is_ctx, ctx_ref[b], x_ref[b])
        h = _norm_mod(xb, g, sc_ref[0, b:b + 1, :], sh_ref[0, b:b + 1, :])
        for s in range(d // LANES):
            slab_ref[s, pl.ds(b, tt, stride=nb), :] = h[:, s * LANES:(s + 1) * LANES]
    hfull = jnp.concatenate([slab_ref[s] for s in range(d // LANES)], axis=1).astype(BF16)
    for n in range(0, 2 * w, nch):
        res = _dot(hfull, w_ref[:, n:n + nch])
        if n < w:
            u_ref[:, n:n + nch] = res.astype(BF16)
        else:
            gs_ref[:, n - w:n - w + nch] = _silu(res).astype(BF16)


def _lru_in(ctx, x, shift2, scale2, g, w_in):
    nb, c, d = ctx.shape
    s = x.shape[1]
    w = w_in.shape[1] // 2
    tt = 16
    nc, nl = c // tt, s // tt
    m = tt * nb
    nch = min(512, w)
    kern = functools.partial(_lru_in_kernel, nc=nc, nb=nb, tt=tt, d=d, w=w, nch=nch)
    out = jax.ShapeDtypeStruct(((c + s) * nb, w), BF16)
    return pl.pallas_call(
        kern,
        grid=(nc + nl,),
        in_specs=[
            pl.BlockSpec((nb, tt, d), lambda j: (0, jnp.minimum(j, nc - 1), 0)),
            pl.BlockSpec((nb, tt, d), lambda j: (0, jnp.maximum(j - nc, 0), 0)),
            pl.BlockSpec((1, nb, d), lambda j: (jnp.where(j < nc, 0, 1), 0, 0)),
            pl.BlockSpec((1, nb, d), lambda j: (jnp.where(j < nc, 0, 1), 0, 0)),
            pl.BlockSpec((1, d), lambda j: (0, 0)),
            _resident((d, 2 * w)),
        ],
        out_specs=[pl.BlockSpec((m, w), lambda j: (j, 0)), pl.BlockSpec((m, w), lambda j: (j, 0))],
        out_shape=[out, out],
        scratch_shapes=[pltpu.VMEM((d // LANES, m, LANES), F32)],
        compiler_params=_params(("parallel",)),
        name="lru_in",
    )(ctx, x, shift2, scale2, g, w_in)


def _scan_chunk(j, nc, ntot, reverse):
    if not reverse:
        return j
    return jnp.where(j < nc, nc - 1 - j, ntot - 1 - j + nc)


def _lru_scan_kernel(prev_ref, cur_ref, next_ref, cw_ref, cb_ref, gw_ref, lam_ref, hs_ref,
                     a_ref, b_ref, h_ref, *, nc, ntot, nb, tc, wb, reverse):
    j = pl.program_id(1)
    c = _scan_chunk(j, nc, ntot, reverse)
    seq_start = (c == 0) | (c == nc)
    seq_end = (c == nc - 1) | (c == ntot - 1)
    m = tc * nb

    @pl.when(j == 0)
    def _():
        h_ref[...] = jnp.zeros_like(h_ref)

    prev = jnp.where(seq_start, 0.0, prev_ref[...].astype(F32))
    nxt = jnp.where(seq_end, 0.0, next_ref[0:nb, :].astype(F32))
    ext = jnp.concatenate([prev, cur_ref[...].astype(F32), nxt], axis=0)
    cw = cw_ref[...]
    xc = cb_ref[...] + cw[0:1] * ext[0:m]
    for k in range(1, 4):
        xc = xc + cw[k:k + 1] * ext[k * nb:k * nb + m]

    lam = lam_ref[...]
    half_c = (0.5 * LRU_C) * (jnp.maximum(-lam, 0.0) + jnp.log1p(jnp.exp(-jnp.abs(lam))))
    nslab = wb // LANES
    ones = jnp.ones((m, LANES), BF16)
    for s in range(nslab):
        xs = xc[:, s * LANES:(s + 1) * LANES]
        th = jnp.tanh(_dot(jnp.concatenate([xs.astype(BF16), ones], axis=1), gw_ref[s]))
        hc = half_c[:, s * LANES:(s + 1) * LANES]
        nla = th[:, :LANES] * hc + hc
        a = jnp.exp2(nla * (-math.log2(math.e)))
        a_ref[s] = a
        y = jnp.tanh(nla) * (a * a + 1.0)
        sq = jnp.where(y > 0.0, y * lax.rsqrt(y), 0.0)
        b_ref[s] = sq * ((0.5 * th[:, LANES:] + 0.5) * xs)

    def step(i, hs):
        t = (tc - 1 - i) if reverse else i
        rows = pl.ds(pl.multiple_of(t * nb, nb), nb)
        out = []
        for s in range(nslab):
            h = a_ref[s, rows, :] * hs[s] + b_ref[s, rows, :]
            hs_ref[rows, s * LANES:(s + 1) * LANES] = h
            out.append(h)
        return tuple(out)

    hs = lax.fori_loop(0, tc, step, tuple(h_ref[s] for s in range(nslab)), unroll=4)
    for s in range(nslab):
        h_ref[s] = hs[s]


def _lru_scan(u_tm, conv_w, conv_b, gw, lam, *, nb, c, reverse):
    rows, w = u_tm.shape
    ltot = rows // nb
    tc = math.gcd(c, 64)
    wb = min(512, w)
    nc, ntot = c // tc, ltot // tc
    m = tc * nb
    hb = tc // 2
    last_halo = ltot // 2 - 1
    chunk = functools.partial(_scan_chunk, nc=nc, ntot=ntot, reverse=reverse)
    kern = functools.partial(_lru_scan_kernel, nc=nc, ntot=ntot, nb=nb, tc=tc, wb=wb, reverse=reverse)
    nslab = wb // LANES
    return pl.pallas_call(
        kern,
        grid=(w // wb, ntot),
        in_specs=[
            pl.BlockSpec((2 * nb, wb), lambda i, j: (jnp.maximum(chunk(j) * hb - 1, 0), i)),
            pl.BlockSpec((m, wb), lambda i, j: (chunk(j), i)),
            pl.BlockSpec((2 * nb, wb), lambda i, j: (jnp.minimum((chunk(j) + 1) * hb, last_halo), i)),
            pl.BlockSpec((4, wb), lambda i, j: (0, i)),
            pl.BlockSpec((1, wb), lambda i, j: (0, i)),
            pl.BlockSpec((nslab, 2 * LANES, 2 * LANES), lambda i, j: (i, 0, 0)),
            pl.BlockSpec((1, wb), lambda i, j: (0, i)),
        ],
        out_specs=pl.BlockSpec((m, wb), lambda i, j: (chunk(j), i)),
        out_shape=jax.ShapeDtypeStruct((rows, w), F32),
        scratch_shapes=[
            pltpu.VMEM((nslab, m, LANES), F32),
            pltpu.VMEM((nslab, m, LANES), F32),
            pltpu.VMEM((nslab, nb, LANES), F32),
        ],
        compiler_params=_params(("parallel", "arbitrary")),
        name="lru_scan_bwd" if reverse else "lru_scan_fwd",
    )(u_tm, u_tm, u_tm, conv_w, conv_b, gw, lam)


def _lru_out_kernel(hf_ref, hb_ref, gs_ref, w_ref, x_ref, gate_ref, g_ref, sh_ref, sc_ref, x1_ref, h1_ref, slab_ref,
                    *, nb, tt, d):
    z = ((hf_ref[...] + hb_ref[...]) * gs_ref[...].astype(F32)).astype(BF16)
    y = _dot(z, w_ref[...])
    nslab = d // LANES
    for s in range(nslab):
        slab_ref[s] = y[:, s * LANES:(s + 1) * LANES]
    g = g_ref[...]
    for b in range(nb):
        yb = jnp.concatenate([slab_ref[s, pl.ds(b, tt, stride=nb), :] for s in range(nslab)], axis=1)
        x1 = x_ref[b] + gate_ref[b] * yb
        x1_ref[b] = x1
        h1_ref[b] = _norm_mod(x1, g, sc_ref[b], sh_ref[b]).astype(BF16)


def _lru_out(hs_f, hs_b, gs, w_out, xres, gate, g1, shift1, scale1, *, row_off):
    nb, l, d = xres.shape
    w = w_out.shape[0]
    tt = 16
    m = tt * nb
    off = row_off // tt
    kern = functools.partial(_lru_out_kernel, nb=nb, tt=tt, d=d)
    bmod = pl.BlockSpec((nb, 1, d), lambda j: (0, 0, 0))
    return pl.pallas_call(
        kern,
        grid=(l // tt,),
        in_specs=[
            pl.BlockSpec((m, w), lambda j: (j + off, 0)),
            pl.BlockSpec((m, w), lambda j: (j + off, 0)),
            pl.BlockSpec((m, w), lambda j: (j + off, 0)),
            _resident((w, d)),
            pl.BlockSpec((nb, tt, d), lambda j: (0, j, 0)),
            bmod,
            pl.BlockSpec((1, d), lambda j: (0, 0)),
            bmod,
            bmod,
        ],
        out_specs=[pl.BlockSpec((nb, tt, d), lambda j: (0, j, 0)), pl.BlockSpec((nb, tt, d), lambda j: (0, j, 0))],
        out_shape=[jax.ShapeDtypeStruct((nb, l, d), F32), jax.ShapeDtypeStruct((nb, l, d), BF16)],
        scratch_shapes=[pltpu.VMEM((d // LANES, m, LANES), F32)],
        compiler_params=_params(("parallel",)),
        name="lru_out",
    )(hs_f, hs_b, gs, w_out, xres, gate, g1, shift1, scale1)


def _att_proj_kernel(*refs, mode, rope, nheads, aliased):
    refs = list(refs)
    h_ref, w_ref = refs[0], refs[1]
    pos = 2
    if mode in ("q", "k"):
        gain_ref, gmat_ref = refs[pos], refs[pos + 1]
        pos += 2
        if rope:
            cos_ref, sa_ref, sb_ref = refs[pos:pos + 3]
            pos += 3
    if aliased:
        pos += 1
    outs = refs[pos:]
    if mode == "v":
        res_t = lax.dot_general(w_ref[...], h_ref[0], _NT, preferred_element_type=F32)
        ones = jnp.ones((V_ROWS - LANES, res_t.shape[1]), BF16)
        for h in range(nheads):
            outs[0][0, h, 0:LANES, :] = res_t[h * LANES:(h + 1) * LANES, :].astype(BF16)
            outs[0][0, h, LANES:V_ROWS, :] = ones
        return
    if mode == "g":
        outs[0][0] = _silu(_dot(h_ref[0], w_ref[...])).astype(BF16)
        return
    res_ref = outs.pop()

    @pl.when(pl.program_id(0) == 0)
    def _():
        res_ref[...] = jnp.zeros_like(res_ref)

    for p in range(res_ref.shape[1] // MXU_DIM):
        r = res_ref[:, p * MXU_DIM:(p + 1) * MXU_DIM]
        ssq = _dot((r * r).astype(BF16), gmat_ref[...])
        rn = r * lax.rsqrt(ssq * (1.0 / HEAD_DIM) + EPS) * gain_ref[:, p * MXU_DIM:(p + 1) * MXU_DIM]
        for hh in range(MXU_DIM // LANES):
            xh = rn[:, hh * LANES:(hh + 1) * LANES]
            if rope:
                xh = (xh * cos_ref[...] + pltpu.roll(xh, LANES - ROPE_FREQS, 1) * sa_ref[...]
                      + pltpu.roll(xh, ROPE_FREQS, 1) * sb_ref[...])
            outs[0][0, p * (MXU_DIM // LANES) + hh] = xh.astype(BF16)
    res_ref[...] = _dot(h_ref[0], w_ref[...])


def _att_proj(h1, w, *, mode, gain=None, gmat=None, rope_tabs=None, out_len=None, out_off=0, into=None):
    nb, l, d = h1.shape
    n = w.shape[0] if mode == "v" else w.shape[1]
    tr = min(512, l)
    nheads = n // LANES
    out_len = l if out_len is None else out_len
    off = out_off // tr
    rope = rope_tabs is not None
    nt = l // tr
    ntiles = nb * nt
    lag = 1 if mode in ("q", "k") else 0

    def cur(j):
        t = jnp.minimum(j, ntiles - 1)
        return t // nt, t % nt

    def done(j):
        t = jnp.maximum(j - lag, 0)
        return t // nt, t % nt

    const = lambda j: (0, 0)
    args = [h1, w]
    in_specs = [pl.BlockSpec((1, tr, d), lambda j: (*cur(j), 0)), _resident(w.shape)]
    scratch = []
    if mode in ("q", "k"):
        args += [gain, gmat]
        in_specs += [pl.BlockSpec((1, n), const), pl.BlockSpec((MXU_DIM, MXU_DIM), const)]
        scratch = [pltpu.VMEM((tr, n), F32)]
        if rope:
            args += list(rope_tabs)
            in_specs += [pl.BlockSpec((tr, LANES), lambda j: (done(j)[1], 0))] * 3
    aliases = {}
    if into is not None:
        aliases = {len(args): 0}
        args.append(into)
        in_specs.append(pl.BlockSpec(memory_space=pl.ANY))
    if mode == "g":
        out_specs = [pl.BlockSpec((1, tr, n), lambda j: (*done(j), 0))]
        out_shape = [jax.ShapeDtypeStruct((nb, l, n), BF16)]
    elif mode == "v":
        out_specs = [pl.BlockSpec((1, nheads, V_ROWS, tr), lambda j: (done(j)[0], 0, 0, done(j)[1] + off))]
        out_shape = [jax.ShapeDtypeStruct((nb, nheads, V_ROWS, out_len), BF16)]
    else:
        hspec = pl.BlockSpec((1, nheads, tr, LANES), lambda j: (done(j)[0], 0, done(j)[1] + off, 0))
        hshape = jax.ShapeDtypeStruct((nb, nheads, out_len, LANES), BF16)
        out_specs, out_shape = [hspec], [hshape]
    kern = functools.partial(_att_proj_kernel, mode=mode, rope=rope, nheads=nheads, aliased=into is not None)
    return pl.pallas_call(
        kern,
        grid=(ntiles + lag,),
        in_specs=in_specs,
        out_specs=out_specs,
        out_shape=out_shape,
        scratch_shapes=scratch,
        input_output_aliases=aliases,
        compiler_params=_params(("arbitrary",)),
        name="att_proj_" + mode + ("_rope" if rope else ""),
    )(*args)


def _attn_kernel(q_ref, k_ref, vt_ref, lv_ref, sg_ref, o_ref, st_ref, m_ref, p_ref, *, lam_init, tq):
    @pl.when(pl.program_id(0) == 0)
    def _():
        st_ref[...] = jnp.zeros_like(st_ref)
        m_ref[...] = jnp.zeros_like(m_ref)
        p_ref[...] = jnp.ones_like(p_ref)

    lv = lv_ref[...]
    lam = (jnp.exp(jnp.sum(lv[0:1] * lv[1:2], axis=-1, keepdims=True))
           - jnp.exp(jnp.sum(lv[2:3] * lv[3:4], axis=-1, keepdims=True)) + lam_init)
    ot = _dot(vt_ref[0, 0], p_ref[...])
    on = ot[0:LANES] * (1.0 / ot[LANES:LANES + 1])
    od = on[:, 0:tq] - lam * on[:, tq:2 * tq]
    ms = jnp.mean(od * od, axis=0, keepdims=True)
    o_ref[0, 0] = (od * lax.rsqrt(ms + EPS) * sg_ref[...]).T.astype(BF16)

    p_ref[...] = jnp.exp2((st_ref[...] - m_ref[...]).astype(BF16))

    qh = q_ref[0, 0]
    first = lax.broadcasted_iota(jnp.int32, qh.shape, 1) < HEAD_DIM
    zero = jnp.zeros_like(qh)
    q = jnp.concatenate([jnp.where(first, qh, zero), jnp.where(first, zero, qh)], axis=0)
    st = lax.dot_general(k_ref[0, 0], q, _NT, preferred_element_type=F32)
    st_ref[...] = st
    m_ref[...] = jnp.max(st, axis=0, keepdims=True)


def _attention(qh, kh, vt, lam_vecs, subln_t, *, lam_init):
    nb, nh, s, _ = qh.shape
    ltot = kh.shape[2]
    tq = subln_t.shape[1]
    nq = s // tq
    ntiles = nb * nh * nq

    def tile(t):
        return t // (nh * nq), (t // nq) % nh, t % nq

    def q_map(i):
        b, h, qi = tile(jnp.minimum(i, ntiles - 1))
        return b, h, qi, 0

    def k_map(i):
        b, h, _ = tile(jnp.minimum(i, ntiles - 1))
        return b, h, 0, 0

    def v_map(i):
        b, h, _ = tile(jnp.maximum(i - 2, 0))
        return b, h, 0, 0

    def o_map(i):
        b, h, qi = tile(jnp.maximum(i - 2, 0))
        return b, h, qi, 0

    qspec = pl.BlockSpec((1, 1, tq, LANES), q_map)
    return pl.pallas_call(
        functools.partial(_attn_kernel, lam_init=lam_init, tq=tq),
        grid=(ntiles + 2,),
        in_specs=[qspec,
                  pl.BlockSpec((1, 1, ltot, LANES), k_map),
                  pl.BlockSpec((1, 1, V_ROWS, ltot), v_map),
                  pl.BlockSpec((4, HEAD_DIM), lambda i: (0, 0)),
                  pl.BlockSpec((LANES, tq), lambda i: (0, 0))],
        out_specs=pl.BlockSpec((1, 1, tq, LANES), o_map),
        out_shape=jax.ShapeDtypeStruct((nb, nh, s, LANES), BF16),
        scratch_shapes=[pltpu.VMEM((ltot, 2 * tq), F32), pltpu.VMEM((1, 2 * tq), F32),
                        pltpu.VMEM((ltot, 2 * tq), BF16)],
        compiler_params=_params(("arbitrary",)),
        name="diff_attention",
    )(qh, kh, vt, lam_vecs, subln_t)


def _att_out_kernel(o_ref, gs_ref, w_ref, x_ref, gate_ref, out_ref, *, nheads):
    o = jnp.concatenate([o_ref[0, h] for h in range(nheads)], axis=1)
    z = (o.astype(F32) * gs_ref[0].astype(F32)).astype(BF16)
    out_ref[0] = x_ref[0] + gate_ref[0] * _dot(z, w_ref[...])


def _att_out(o_heads, gs, w_out, x1, gate):
    nb, nh, s, _ = o_heads.shape
    d = w_out.shape[1]
    wd = w_out.shape[0]
    tr = min(512, s)
    return pl.pallas_call(
        functools.partial(_att_out_kernel, nheads=nh),
        grid=(nb, s // tr),
        in_specs=[
            pl.BlockSpec((1, nh, tr, LANES), lambda b, i: (b, 0, i, 0)),
            pl.BlockSpec((1, tr, wd), lambda b, i: (b, i, 0)),
            _resident((wd, d)),
            pl.BlockSpec((1, tr, d), lambda b, i: (b, i, 0)),
            pl.BlockSpec((1, 1, d), lambda b, i: (b, 0, 0)),
        ],
        out_specs=pl.BlockSpec((1, tr, d), lambda b, i: (b, i, 0)),
        out_shape=jax.ShapeDtypeStruct((nb, s, d), F32),
        compiler_params=_params(("parallel", "parallel")),
        name="att_out",
    )(o_heads, gs, w_out, x1, gate)


def _gate_weights(gw, gb):
    w = 0.5 * jnp.concatenate([gw[0], gw[1]], axis=-1)
    b = 0.5 * jnp.concatenate([gb[0], gb[1]], axis=-1)
    b_hi = b.astype(BF16)
    b_lo = (b - b_hi.astype(F32)).astype(BF16)
    pad = jnp.zeros((w.shape[0], LANES - 2, 2 * LANES), BF16)
    return jnp.concatenate([w.astype(BF16), b_hi[:, None, :], b_lo[:, None, :], pad], axis=1)


def _rope_tables(s):
    t = jnp.arange(s)
    pos = jnp.stack([(t // GRID_W).astype(F32), (t % GRID_W).astype(F32)], axis=1)
    inv = ROPE_BASE ** (-jnp.arange(ROPE_FREQS, dtype=F32) / ROPE_FREQS)
    ang = pos[:, :, None] * inv
    lane = jnp.arange(LANES)
    dd = lane % HEAD_DIM
    axis, half, f = dd // (2 * ROPE_FREQS), (dd % (2 * ROPE_FREQS)) // ROPE_FREQS, dd % ROPE_FREQS
    cos = jnp.cos(ang)[:, axis, f]
    sin = jnp.sin(ang)[:, axis, f]
    return cos, jnp.where(half == 0, -sin, 0.0), jnp.where(half == 1, sin, 0.0)


def kernel(x, c, ctx, c_ctx, mod_w, mod_b, norm_g, lru_w_in, lru_conv_w, lru_conv_b, lru_gate_w, lru_gate_b,
           lru_lambda, lru_w_out, att_w_in, att_q_norm, att_k_norm, att_lambda, att_subln, att_w_out):
    nb, s, d = x.shape
    clen = ctx.shape[1]
    w = lru_w_in.shape[-1] // 2
    aw = att_w_out.shape[1]
    nheads = aw // LANES
    assert lru_gate_w.shape[-1] == LANES and d % MXU_DIM == 0 and nb % 8 == 0

    rows = ((nb + 1 + 7) // 8) * 8
    cs = jnp.zeros((rows, d), F32).at[:nb].set(c).at[nb].set(c_ctx)
    mod = _modulation(cs, mod_w, mod_b)

    def split(i):
        lat = [mod[i, :nb, k * d:(k + 1) * d] for k in range(3)]
        cx = [jnp.broadcast_to(mod[i, nb:nb + 1, k * d:(k + 1) * d], (nb, d)) for k in range(3)]
        return lat, cx

    (shift0, scale0, gate0), (cshift0, cscale0, cgate0) = split(0)
    (shift1, scale1, gate1), (cshift1, cscale1, _) = split(1)
    col = lambda a: a.reshape(nb, 1, d)

    u_tm, gs_tm = _lru_in(ctx, x, jnp.stack([cshift0, shift0]), jnp.stack([cscale0, scale0]),
                          norm_g[0:1], lru_w_in[0].astype(BF16))
    hs = []
    for dr in range(2):
        hs.append(_lru_scan(u_tm, lru_conv_w[0], lru_conv_b[0:1], _gate_weights(lru_gate_w[0, dr], lru_gate_b[0, dr]),
                            lru_lambda[0, dr:dr + 1], nb=nb, c=clen, reverse=bool(dr)))
    w_out0 = lru_w_out[0].astype(BF16)
    g1 = norm_g[1:2]
    _, h1_ctx = _lru_out(hs[0], hs[1], gs_tm, w_out0, ctx, col(cgate0), g1, col(cshift1), col(cscale1), row_off=0)
    x1, h1_lat = _lru_out(hs[0], hs[1], gs_tm, w_out0, x, col(gate0), g1, col(shift1), col(scale1), row_off=clen)

    lam_init = 0.8 - 0.6 * math.exp(-0.3 * 1)
    wq, wk, wv, wg = (att_w_in[0][:, k * aw:(k + 1) * aw].astype(BF16) for k in range(4))
    gmat = jnp.kron(jnp.eye(MXU_DIM // HEAD_DIM, dtype=F32), jnp.ones((HEAD_DIM, HEAD_DIM), F32)).astype(BF16)
    reps = aw // HEAD_DIM
    q_gain = jnp.tile(att_q_norm[0] * (HEAD_DIM ** -0.5 * math.log2(math.e)), reps)[None, :]
    k_gain = jnp.tile(att_k_norm[0], reps)[None, :]
    tabs = _rope_tables(s)
    ltot = s + clen
    qh, = _att_proj(h1_lat, wq, mode="q", gain=q_gain, gmat=gmat, rope_tabs=tabs)
    kh, = _att_proj(h1_lat, wk, mode="k", gain=k_gain, gmat=gmat, rope_tabs=tabs, out_len=ltot)
    kh, = _att_proj(h1_ctx, wk, mode="k", gain=k_gain, gmat=gmat, out_len=ltot, out_off=s, into=kh)
    vt, = _att_proj(h1_lat, wv.T, mode="v", out_len=ltot)
    vt, = _att_proj(h1_ctx, wv.T, mode="v", out_len=ltot, out_off=s, into=vt)
    gsa, = _att_proj(h1_lat, wg, mode="g")
    subln_t = jnp.broadcast_to((att_subln[0] * (1.0 - lam_init))[:, None], (LANES, min(ATT_TQ, s)))
    o_heads = _attention(qh, kh, vt, att_lambda[0], subln_t, lam_init=lam_init)
    return _att_out(o_heads, gsa, att_w_out[0].astype(BF16), x1, col(gate1))
```

```python
import functools
import math

import jax
import jax.numpy as jnp
from jax import lax
from jax.experimental import pallas as pl
from jax.experimental.pallas import tpu as pltpu

F32 = jnp.float32
BF16 = jnp.bfloat16

EPS = 1e-6
LRU_C = 8.0
GRID_W = 64
HEAD_DIM = 64
ROPE_FREQS = HEAD_DIM // 4
ROPE_BASE = 10000.0
LANES = 128
MXU_DIM = 256
VMEM_LIMIT = 56 * 1024 * 1024
V_ROWS = LANES + 16
ATT_TQ = 256
LRU_TT = 32
LRU_SUB = 1
_NT = (((1,), (1,)), ((), ()))


def _params(sem, vmem=VMEM_LIMIT):
    return pltpu.CompilerParams(dimension_semantics=sem, vmem_limit_bytes=vmem)


def _sigmoid(x):
    return 1.0 / (1.0 + jnp.exp(-x))


def _silu(x):
    return x * _sigmoid(x)


def _dot(a, b):
    return jnp.dot(a, b, preferred_element_type=F32)


def _resident(shape):
    nd = len(shape)
    return pl.BlockSpec(shape, lambda *_: (0,) * nd, pipeline_mode=pl.Buffered(1))


def _mod_kernel(cs_ref, w_ref, b_ref, o_ref):
    a = _silu(cs_ref[...])
    w = w_ref[0]
    a_hi = a.astype(BF16)
    a_lo = (a - a_hi.astype(F32)).astype(BF16)
    w_hi = w.astype(BF16)
    w_lo = (w - w_hi.astype(F32)).astype(BF16)
    acc = _dot(a_hi, w_hi) + _dot(a_lo, w_hi) + _dot(a_hi, w_lo)
    o_ref[0] = acc + b_ref[0]


def _modulation(cs, mod_w, mod_b):
    depth, d, n3 = mod_w.shape
    rows = cs.shape[0]
    tn = 768 if n3 % 768 == 0 else n3
    return pl.pallas_call(
        _mod_kernel,
        grid=(depth, n3 // tn),
        in_specs=[
            pl.BlockSpec((rows, d), lambda i, n: (0, 0)),
            pl.BlockSpec((1, d, tn), lambda i, n: (i, 0, n)),
            pl.BlockSpec((1, 1, tn), lambda i, n: (i, 0, n)),
        ],
        out_specs=pl.BlockSpec((1, rows, tn), lambda i, n: (i, 0, n)),
        out_shape=jax.ShapeDtypeStruct((depth, rows, n3), F32),
        compiler_params=_params(("parallel", "parallel")),
        name="modulation",
    )(cs, mod_w, mod_b.reshape(depth, 1, n3))


def _norm_mod(xb, g, scale, shift):
    ms = jnp.mean(xb * xb, axis=-1, keepdims=True)
    return (xb * lax.rsqrt(ms + EPS) * g) * (1.0 + scale) + shift


def _slab_pitch(nb):
    return nb + 8


def _lru_in_kernel(x_ref, sh_ref, sc_ref, g_ref, w_ref, *rest, nb, tt, sub, d, w, nch):
    u_ref, gs_ref = rest[-2 - sub:-sub]
    slabs = rest[-sub:]
    m = tt * nb
    pitch = _slab_pitch(nb)
    g = g_ref[...]
    for k in range(sub):
        slab_ref = slabs[k]
        for b in range(nb):
            h = _norm_mod(x_ref[b, k * tt:(k + 1) * tt, :], g, sc_ref[b], sh_ref[b])
            for s in range(d // LANES):
                slab_ref[s, pl.ds(b, tt, stride=pitch), :] = h[:, s * LANES:(s + 1) * LANES]
        hfull = jnp.concatenate(
            [jnp.concatenate([slab_ref[s, t * pitch:t * pitch + nb, :] for t in range(tt)], axis=0)
             for s in range(d // LANES)], axis=1).astype(BF16)
        rows = slice(k * m, (k + 1) * m)
        for n in range(0, 2 * w, nch):
            res = _dot(hfull, w_ref[:, n:n + nch])
            if n < w:
                u_ref[rows, n:n + nch] = res.astype(BF16)
            else:
                gs_ref[rows, n - w:n - w + nch] = _silu(res).astype(BF16)


def _lru_in(xseq, shift, scale, g, w_in, *, total_len, row_off, into=None):
    nb, l, d = xseq.shape
    w = w_in.shape[1] // 2
    tt, sub = LRU_TT, LRU_SUB
    step = sub * tt
    assert l % step == 0 and row_off % step == 0
    m = tt * nb
    off = row_off // step
    nch = min(512, w)
    kern = functools.partial(_lru_in_kernel, nb=nb, tt=tt, sub=sub, d=d, w=w, nch=nch)
    bmod = pl.BlockSpec((nb, 1, d), lambda j: (0, 0, 0))
    args = [xseq, shift, scale, g, w_in]
    in_specs = [pl.BlockSpec((nb, step, d), lambda j: (0, j, 0)), bmod, bmod, pl.BlockSpec((1, d), lambda j: (0, 0)),
                _resident((d, 2 * w))]
    aliases = {}
    if into is not None:
        aliases = {len(args): 0, len(args) + 1: 1}
        args += list(into)
        in_specs += [pl.BlockSpec(memory_space=pl.ANY)] * 2
    out = jax.ShapeDtypeStruct((total_len * nb, w), BF16)
    tiles = pl.BlockSpec((sub * m, w), lambda j: (j + off, 0))
    return pl.pallas_call(
        kern,
        grid=(l // step,),
        in_specs=in_specs,
        out_specs=[tiles, tiles],
        out_shape=[out, out],
        scratch_shapes=[pltpu.VMEM((d // LANES, tt * _slab_pitch(nb), LANES), F32)] * sub,
        input_output_aliases=aliases,
        compiler_params=_params(("parallel",)),
        name="lru_in",
    )(*args)


def _scan_chunk(j, nc, ntot, reverse):
    if not reverse:
        return j
    return jnp.where(j < nc, nc - 1 - j, ntot - 1 - j + nc)


def _lru_scan_kernel(prev_ref, cur_ref, next_ref, cw_ref, cb_ref, gw_ref, lam_ref, hs_ref,
                     a_ref, b_ref, h_ref, *, nc, ntot, nb, tc, wb, reverse):
    j = pl.program_id(1)
    c = _scan_chunk(j, nc, ntot, reverse)
    seq_start = (c == 0) | (c == nc)
    seq_end = (c == nc - 1) | (c == ntot - 1)
    m = tc * nb

    @pl.when(j == 0)
    def _():
        h_ref[...] = jnp.zeros_like(h_ref)

    prev = jnp.where(seq_start, 0.0, prev_ref[...].astype(F32))
    nxt = jnp.where(seq_end, 0.0, next_ref[0:nb, :].astype(F32))
    ext = jnp.concatenate([prev, cur_ref[...].astype(F32), nxt], axis=0)
    cw = cw_ref[...]
    xc = cb_ref[...] + cw[0:1] * ext[0:m]
    for k in range(1, 4):
        xc = xc + cw[k:k + 1] * ext[k * nb:k * nb + m]

    lam = lam_ref[...]
    half_c = (0.5 * LRU_C) * (jnp.maximum(-lam, 0.0) + jnp.log1p(jnp.exp(-jnp.abs(lam))))
    nslab = wb // LANES
    ones = jnp.ones((m, LANES), BF16)
    for s in range(nslab):
        xs = xc[:, s * LANES:(s + 1) * LANES]
        th = jnp.tanh(_dot(jnp.concatenate([xs.astype(BF16), ones], axis=1), gw_ref[s]))
        hc = half_c[:, s * LANES:(s + 1) * LANES]
        nla = th[:, :LANES] * hc + hc
        a = jnp.exp2(nla * (-math.log2(math.e)))
        a_ref[s] = a
        y = jnp.tanh(nla) * (a * a + 1.0)
        sq = jnp.where(y > 0.0, y * lax.rsqrt(y), 0.0)
        b_ref[s] = sq * ((0.5 * th[:, LANES:] + 0.5) * xs)

    def step(i, hs):
        t = (tc - 1 - i) if reverse else i
        rows = pl.ds(pl.multiple_of(t * nb, nb), nb)
        out = []
        for s in range(nslab):
            h = a_ref[s, rows, :] * hs[s] + b_ref[s, rows, :]
            hs_ref[rows, s * LANES:(s + 1) * LANES] = h.astype(BF16)
            out.append(h)
        return tuple(out)

    hs = lax.fori_loop(0, tc, step, tuple(h_ref[s] for s in range(nslab)), unroll=4)
    for s in range(nslab):
        h_ref[s] = hs[s]


def _lru_scan(u_tm, conv_w, conv_b, gw, lam, *, nb, c, reverse):
    rows, w = u_tm.shape
    ltot = rows // nb
    tc = math.gcd(c, 64)
    wb = min(512, w)
    nc, ntot = c // tc, ltot // tc
    m = tc * nb
    hb = tc // 2
    last_halo = ltot // 2 - 1
    chunk = functools.partial(_scan_chunk, nc=nc, ntot=ntot, reverse=reverse)
    kern = functools.partial(_lru_scan_kernel, nc=nc, ntot=ntot, nb=nb, tc=tc, wb=wb, reverse=reverse)
    nslab = wb // LANES
    return pl.pallas_call(
        kern,
        grid=(w // wb, ntot),
        in_specs=[
            pl.BlockSpec((2 * nb, wb), lambda i, j: (jnp.maximum(chunk(j) * hb - 1, 0), i)),
            pl.BlockSpec((m, wb), lambda i, j: (chunk(j), i)),
            pl.BlockSpec((2 * nb, wb), lambda i, j: (jnp.minimum((chunk(j) + 1) * hb, last_halo), i)),
            pl.BlockSpec((4, wb), lambda i, j: (0, i)),
            pl.BlockSpec((1, wb), lambda i, j: (0, i)),
            pl.BlockSpec((nslab, 2 * LANES, 2 * LANES), lambda i, j: (i, 0, 0)),
            pl.BlockSpec((1, wb), lambda i, j: (0, i)),
        ],
        out_specs=pl.BlockSpec((m, wb), lambda i, j: (chunk(j), i)),
        out_shape=jax.ShapeDtypeStruct((rows, w), BF16),
        scratch_shapes=[
            pltpu.VMEM((nslab, m, LANES), F32),
            pltpu.VMEM((nslab, m, LANES), F32),
            pltpu.VMEM((nslab, nb, LANES), F32),
        ],
        compiler_params=_params(("parallel", "arbitrary")),
        name="lru_scan_bwd" if reverse else "lru_scan_fwd",
    )(u_tm, u_tm, u_tm, conv_w, conv_b, gw, lam)


def _lru_out_kernel(hf_ref, hb_ref, gs_ref, w_ref, x_ref, gate_ref, g_ref, sh_ref, sc_ref, x1_ref, h1_ref,
                    *slabs, nb, tt, d, nch):
    m = tt * nb
    per = nch // LANES
    pitch = _slab_pitch(nb)
    g = g_ref[...]
    for k, slab_ref in enumerate(slabs):
        rows = slice(k * m, (k + 1) * m)
        z = ((hf_ref[rows, :].astype(F32) + hb_ref[rows, :].astype(F32)) * gs_ref[rows, :].astype(F32)).astype(BF16)
        for c in range(d // nch):
            y = _dot(z, w_ref[:, c * nch:(c + 1) * nch])
            for j in range(per):
                for t in range(tt):
                    slab_ref[c * per + j, t * pitch:t * pitch + nb, :] = y[t * nb:(t + 1) * nb, j * LANES:(j + 1) * LANES]
        t0 = k * tt
        for b in range(nb):
            yb = jnp.concatenate([slab_ref[s, pl.ds(b, tt, stride=pitch), :] for s in range(d // LANES)], axis=1)
            x1 = x_ref[b, t0:t0 + tt, :] + gate_ref[b] * yb
            x1_ref[b, t0:t0 + tt, :] = x1
            h1_ref[b, t0:t0 + tt, :] = _norm_mod(x1, g, sc_ref[b], sh_ref[b]).astype(BF16)


def _lru_out(hs_f, hs_b, gs, w_out, xres, gate, g1, shift1, scale1, *, row_off):
    nb, l, d = xres.shape
    w = w_out.shape[0]
    tt, sub = LRU_TT, LRU_SUB
    step = sub * tt
    assert l % step == 0 and row_off % step == 0
    m = tt * nb
    off = row_off // step
    kern = functools.partial(_lru_out_kernel, nb=nb, tt=tt, d=d, nch=min(512, d))
    bmod = pl.BlockSpec((nb, 1, d), lambda j: (0, 0, 0))
    act = pl.BlockSpec((sub * m, w), lambda j: (j + off, 0))
    res = pl.BlockSpec((nb, step, d), lambda j: (0, j, 0))
    return pl.pallas_call(
        kern,
        grid=(l // step,),
        in_specs=[act, act, act, _resident((w, d)), res, bmod, pl.BlockSpec((1, d), lambda j: (0, 0)), bmod, bmod],
        out_specs=[res, res],
        out_shape=[jax.ShapeDtypeStruct((nb, l, d), F32), jax.ShapeDtypeStruct((nb, l, d), BF16)],
        scratch_shapes=[pltpu.VMEM((d // LANES, tt * _slab_pitch(nb), LANES), F32)] * sub,
        compiler_params=_params(("parallel",)),
        name="lru_out",
    )(hs_f, hs_b, gs, w_out, xres, gate, g1, shift1, scale1)


def _att_proj_kernel(*refs, mode, rope, nheads, aliased):
    refs = list(refs)
    h_ref, w_ref = refs[0], refs[1]
    pos = 2
    if mode in ("q", "k"):
        gain_ref, gmat_ref = refs[pos], refs[pos + 1]
        pos += 2
        if rope:
            cos_ref, sa_ref, sb_ref = refs[pos:pos + 3]
            pos += 3
    if aliased:
        pos += 1
    outs = refs[pos:]
    if mode == "v":
        res_t = lax.dot_general(w_ref[...], h_ref[0], _NT, preferred_element_type=F32)
        ones = jnp.ones((V_ROWS - LANES, res_t.shape[1]), BF16)
        for h in range(nheads):
            outs[0][0, h, 0:LANES, :] = res_t[h * LANES:(h + 1) * LANES, :].astype(BF16)
            outs[0][0, h, LANES:V_ROWS, :] = ones
        return
    if mode == "g":
        outs[0][0] = _silu(_dot(h_ref[0], w_ref[...])).astype(BF16)
        return
    res_ref = outs.pop()

    @pl.when(pl.program_id(0) == 0)
    def _():
        res_ref[...] = jnp.zeros_like(res_ref)

    for p in range(res_ref.shape[1] // MXU_DIM):
        r = res_ref[:, p * MXU_DIM:(p + 1) * MXU_DIM]
        ssq = _dot((r * r).astype(BF16), gmat_ref[...])
        rn = r * lax.rsqrt(ssq * (1.0 / HEAD_DIM) + EPS) * gain_ref[:, p * MXU_DIM:(p + 1) * MXU_DIM]
        for hh in range(MXU_DIM // LANES):
            xh = rn[:, hh * LANES:(hh + 1) * LANES]
            if rope:
                xh = (xh * cos_ref[...] + pltpu.roll(xh, LANES - ROPE_FREQS, 1) * sa_ref[...]
                      + pltpu.roll(xh, ROPE_FREQS, 1) * sb_ref[...])
            outs[0][0, p * (MXU_DIM // LANES) + hh] = xh.astype(BF16)
    res_ref[...] = _dot(h_ref[0], w_ref[...])


def _att_proj(h1, w, *, mode, gain=None, gmat=None, rope_tabs=None, out_len=None, out_off=0, into=None):
    nb, l, d = h1.shape
    n = w.shape[0] if mode == "v" else w.shape[1]
    tr = min(512, l)
    nheads = n // LANES
    out_len = l if out_len is None else out_len
    off = out_off // tr
    rope = rope_tabs is not None
    nt = l // tr
    ntiles = nb * nt
    lag = 1 if mode in ("q", "k") else 0

    def cur(j):
        t = jnp.minimum(j, ntiles - 1)
        return t // nt, t % nt

    def done(j):
        t = jnp.maximum(j - lag, 0)
        return t // nt, t % nt

    const = lambda j: (0, 0)
    args = [h1, w]
    in_specs = [pl.BlockSpec((1, tr, d), lambda j: (*cur(j), 0)), _resident(w.shape)]
    scratch = []
    if mode in ("q", "k"):
        args += [gain, gmat]
        in_specs += [pl.BlockSpec((1, n), const), pl.BlockSpec((MXU_DIM, MXU_DIM), const)]
        scratch = [pltpu.VMEM((tr, n), F32)]
        if rope:
            args += list(rope_tabs)
            in_specs += [pl.BlockSpec((tr, LANES), lambda j: (done(j)[1], 0))] * 3
    aliases = {}
    if into is not None:
        aliases = {len(args): 0}
        args.append(into)
        in_specs.append(pl.BlockSpec(memory_space=pl.ANY))
    if mode == "g":
        out_specs = [pl.BlockSpec((1, tr, n), lambda j: (*done(j), 0))]
        out_shape = [jax.ShapeDtypeStruct((nb, l, n), BF16)]
    elif mode == "v":
        out_specs = [pl.BlockSpec((1, nheads, V_ROWS, tr), lambda j: (done(j)[0], 0, 0, done(j)[1] + off))]
        out_shape = [jax.ShapeDtypeStruct((nb, nheads, V_ROWS, out_len), BF16)]
    else:
        hspec = pl.BlockSpec((1, nheads, tr, LANES), lambda j: (done(j)[0], 0, done(j)[1] + off, 0))
        hshape = jax.ShapeDtypeStruct((nb, nheads, out_len, LANES), BF16)
        out_specs, out_shape = [hspec], [hshape]
    kern = functools.partial(_att_proj_kernel, mode=mode, rope=rope, nheads=nheads, aliased=into is not None)
    return pl.pallas_call(
        kern,
        grid=(ntiles + lag,),
        in_specs=in_specs,
        out_specs=out_specs,
        out_shape=out_shape,
        scratch_shapes=scratch,
        input_output_aliases=aliases,
        compiler_params=_params(("arbitrary",)),
        name="att_proj_" + mode + ("_rope" if rope else ""),
    )(*args)


def _attn_kernel(q_ref, k_ref, vt_ref, lv_ref, sg_ref, o_ref, st_ref, m_ref, p_ref, *, lam_init, tq):
    @pl.when(pl.program_id(0) == 0)
    def _():
        st_ref[...] = jnp.zeros_like(st_ref)
        m_ref[...] = jnp.zeros_like(m_ref)
        p_ref[...] = jnp.ones_like(p_ref)

    lv = lv_ref[...]
    lam = (jnp.exp(jnp.sum(lv[0:1] * lv[1:2], axis=-1, keepdims=True))
           - jnp.exp(jnp.sum(lv[2:3] * lv[3:4], axis=-1, keepdims=True)) + lam_init)
    ot = _dot(vt_ref[0, 0], p_ref[...])
    on = ot[0:LANES] * (1.0 / ot[LANES:LANES + 1])
    od = on[:, 0:tq] - lam * on[:, tq:2 * tq]
    ms = jnp.mean(od * od, axis=0, keepdims=True)
    o_ref[0, 0] = (od * lax.rsqrt(ms + EPS) * sg_ref[...]).T.astype(BF16)

    p_ref[...] = jnp.exp2((st_ref[...] - m_ref[...]).astype(BF16))

    qh = q_ref[0, 0]
    first = lax.broadcasted_iota(jnp.int32, qh.shape, 1) < HEAD_DIM
    zero = jnp.zeros_like(qh)
    q = jnp.concatenate([jnp.where(first, qh, zero), jnp.where(first, zero, qh)], axis=0)
    st = lax.dot_general(k_ref[0, 0], q, _NT, preferred_element_type=F32)
    st_ref[...] = st
    m_ref[...] = jnp.max(st, axis=0, keepdims=True)


def _attention(qh, kh, vt, lam_vecs, subln_t, *, lam_init):
    nb, nh, s, _ = qh.shape
    ltot = kh.shape[2]
    tq = subln_t.shape[1]
    nq = s // tq
    ntiles = nb * nh * nq

    def tile(t):
        return t // (nh * nq), (t // nq) % nh, t % nq

    def q_map(i):
        b, h, qi = tile(jnp.minimum(i, ntiles - 1))
        return b, h, qi, 0

    def k_map(i):
        b, h, _ = tile(jnp.minimum(i, ntiles - 1))
        return b, h, 0, 0

    def v_map(i):
        b, h, _ = tile(jnp.maximum(i - 2, 0))
        return b, h, 0, 0

    def o_map(i):
        b, h, qi = tile(jnp.maximum(i - 2, 0))
        return b, h, qi, 0

    qspec = pl.BlockSpec((1, 1, tq, LANES), q_map)
    return pl.pallas_call(
        functools.partial(_attn_kernel, lam_init=lam_init, tq=tq),
        grid=(ntiles + 2,),
        in_specs=[qspec,
                  pl.BlockSpec((1, 1, ltot, LANES), k_map),
                  pl.BlockSpec((1, 1, V_ROWS, ltot), v_map),
                  pl.BlockSpec((4, HEAD_DIM), lambda i: (0, 0)),
                  pl.BlockSpec((LANES, tq), lambda i: (0, 0))],
        out_specs=pl.BlockSpec((1, 1, tq, LANES), o_map),
        out_shape=jax.ShapeDtypeStruct((nb, nh, s, LANES), BF16),
        scratch_shapes=[pltpu.VMEM((ltot, 2 * tq), F32), pltpu.VMEM((1, 2 * tq), F32),
                        pltpu.VMEM((ltot, 2 * tq), BF16)],
        compiler_params=_params(("arbitrary",)),
        name="diff_attention",
    )(qh, kh, vt, lam_vecs, subln_t)


def _att_out_kernel(o_ref, gs_ref, w_ref, x_ref, gate_ref, out_ref, *, nheads):
    o = jnp.concatenate([o_ref[0, h] for h in range(nheads)], axis=1)
    z = (o.astype(F32) * gs_ref[0].astype(F32)).astype(BF16)
    out_ref[0] = x_ref[0] + gate_ref[0] * _dot(z, w_ref[...])


def _att_out(o_heads, gs, w_out, x1, gate):
    nb, nh, s, _ = o_heads.shape
    d = w_out.shape[1]
    wd = w_out.shape[0]
    tr = min(512, s)
    return pl.pallas_call(
        functools.partial(_att_out_kernel, nheads=nh),
        grid=(nb, s // tr),
        in_specs=[
            pl.BlockSpec((1, nh, tr, LANES), lambda b, i: (b, 0, i, 0)),
            pl.BlockSpec((1, tr, wd), lambda b, i: (b, i, 0)),
            _resident((wd, d)),
            pl.BlockSpec((1, tr, d), lambda b, i: (b, i, 0)),
            pl.BlockSpec((1, 1, d), lambda b, i: (b, 0, 0)),
        ],
        out_specs=pl.BlockSpec((1, tr, d), lambda b, i: (b, i, 0)),
        out_shape=jax.ShapeDtypeStruct((nb, s, d), F32),
        compiler_params=_params(("parallel", "parallel")),
        name="att_out",
    )(o_heads, gs, w_out, x1, gate)


def _gate_weights(gw, gb):
    w = 0.5 * jnp.concatenate([gw[0], gw[1]], axis=-1)
    b = 0.5 * jnp.concatenate([gb[0], gb[1]], axis=-1)
    b_hi = b.astype(BF16)
    b_lo = (b - b_hi.astype(F32)).astype(BF16)
    pad = jnp.zeros((w.shape[0], LANES - 2, 2 * LANES), BF16)
    return jnp.concatenate([w.astype(BF16), b_hi[:, None, :], b_lo[:, None, :], pad], axis=1)


def _rope_tables(s):
    t = jnp.arange(s)
    pos = jnp.stack([(t // GRID_W).astype(F32), (t % GRID_W).astype(F32)], axis=1)
    inv = ROPE_BASE ** (-jnp.arange(ROPE_FREQS, dtype=F32) / ROPE_FREQS)
    ang = pos[:, :, None] * inv
    lane = jnp.arange(LANES)
    dd = lane % HEAD_DIM
    axis, half, f = dd // (2 * ROPE_FREQS), (dd % (2 * ROPE_FREQS)) // ROPE_FREQS, dd % ROPE_FREQS
    cos = jnp.cos(ang)[:, axis, f]
    sin = jnp.sin(ang)[:, axis, f]
    return cos, jnp.where(half == 0, -sin, 0.0), jnp.where(half == 1, sin, 0.0)


def kernel(x, c, ctx, c_ctx, mod_w, mod_b, norm_g, lru_w_in, lru_conv_w, lru_conv_b, lru_gate_w, lru_gate_b,
           lru_lambda, lru_w_out, att_w_in, att_q_norm, att_k_norm, att_lambda, att_subln, att_w_out):
    nb, s, d = x.shape
    clen = ctx.shape[1]
    aw = att_w_out.shape[1]
    assert lru_gate_w.shape[-1] == LANES and d % MXU_DIM == 0 and nb % 8 == 0

    rows = ((nb + 1 + 7) // 8) * 8
    cs = jnp.zeros((rows, d), F32).at[:nb].set(c).at[nb].set(c_ctx)
    mod = _modulation(cs, mod_w, mod_b)

    def split(i):
        lat = [mod[i, :nb, k * d:(k + 1) * d].reshape(nb, 1, d) for k in range(3)]
        cx = [jnp.broadcast_to(mod[i, nb:nb + 1, k * d:(k + 1) * d], (nb, d)).reshape(nb, 1, d) for k in range(3)]
        return lat, cx

    (shift0, scale0, gate0), (cshift0, cscale0, cgate0) = split(0)
    (shift1, scale1, gate1), (cshift1, cscale1, _) = split(1)

    w_in0 = lru_w_in[0].astype(BF16)
    ltot = s + clen
    tm = _lru_in(ctx, cshift0, cscale0, norm_g[0:1], w_in0, total_len=ltot, row_off=0)
    u_tm, gs_tm = _lru_in(x, shift0, scale0, norm_g[0:1], w_in0, total_len=ltot, row_off=clen, into=tm)
    hs = []
    for dr in range(2):
        hs.append(_lru_scan(u_tm, lru_conv_w[0], lru_conv_b[0:1], _gate_weights(lru_gate_w[0, dr], lru_gate_b[0, dr]),
                            lru_lambda[0, dr:dr + 1], nb=nb, c=clen, reverse=bool(dr)))
    w_out0 = lru_w_out[0].astype(BF16)
    g1 = norm_g[1:2]
    _, h1_ctx = _lru_out(hs[0], hs[1], gs_tm, w_out0, ctx, cgate0, g1, cshift1, cscale1, row_off=0)
    x1, h1_lat = _lru_out(hs[0], hs[1], gs_tm, w_out0, x, gate0, g1, shift1, scale1, row_off=clen)

    lam_init = 0.8 - 0.6 * math.exp(-0.3 * 1)
    wq, wk, wv, wg = (att_w_in[0][:, k * aw:(k + 1) * aw].astype(BF16) for k in range(4))
    gmat = jnp.kron(jnp.eye(MXU_DIM // HEAD_DIM, dtype=F32), jnp.ones((HEAD_DIM, HEAD_DIM), F32)).astype(BF16)
    reps = aw // HEAD_DIM
    q_gain = jnp.tile(att_q_norm[0] * (HEAD_DIM ** -0.5 * math.log2(math.e)), reps)[None, :]
    k_gain = jnp.tile(att_k_norm[0], reps)[None, :]
    tabs = _rope_tables(s)
    qh, = _att_proj(h1_lat, wq, mode="q", gain=q_gain, gmat=gmat, rope_tabs=tabs)
    kh, = _att_proj(h1_lat, wk, mode="k", gain=k_gain, gmat=gmat, rope_tabs=tabs, out_len=ltot)
    kh, = _att_proj(h1_ctx, wk, mode="k", gain=k_gain, gmat=gmat, out_len=ltot, out_off=s, into=kh)
    vt, = _att_proj(h1_lat, wv.T, mode="v", out_len=ltot)
    vt, = _att_proj(h1_ctx, wv.T, mode="v", out_len=ltot, out_off=s, into=vt)
    gsa, = _att_proj(h1_lat, wg, mode="g")
    subln_t = jnp.broadcast_to((att_subln[0] * (1.0 - lam_init))[:, None], (LANES, min(ATT_TQ, s)))
    o_heads = _attention(qh, kh, vt, att_lambda[0], subln_t, lam_init=lam_init)
    return _att_out(o_heads, gsa, att_w_out[0].astype(BF16), x1, gate1)
```

```python
import functools
import math

import jax
import jax.numpy as jnp
from jax import lax
from jax.experimental import pallas as pl
from jax.experimental.pallas import tpu as pltpu

F32 = jnp.float32
BF16 = jnp.bfloat16

EPS = 1e-6
LRU_C = 8.0
GRID_W = 64
HEAD_DIM = 64
ROPE_FREQS = HEAD_DIM // 4
ROPE_BASE = 10000.0
LANES = 128
MXU_DIM = 256
VMEM_LIMIT = 56 * 1024 * 1024
V_ROWS = LANES + 16
ATT_TQ = 1024
LRU_TT = 32
LRU_SUB = 1
_NT = (((1,), (1,)), ((), ()))


def _params(sem, vmem=VMEM_LIMIT):
    return pltpu.CompilerParams(dimension_semantics=sem, vmem_limit_bytes=vmem)


def _sigmoid(x):
    return 1.0 / (1.0 + jnp.exp(-x))


def _silu(x):
    return x * _sigmoid(x)


def _dot(a, b):
    return jnp.dot(a, b, preferred_element_type=F32)


def _resident(shape):
    nd = len(shape)
    return pl.BlockSpec(shape, lambda *_: (0,) * nd, pipeline_mode=pl.Buffered(1))


def _mod_kernel(cs_ref, w_ref, b_ref, o_ref):
    a = _silu(cs_ref[...])
    w = w_ref[0]
    a_hi = a.astype(BF16)
    a_lo = (a - a_hi.astype(F32)).astype(BF16)
    w_hi = w.astype(BF16)
    w_lo = (w - w_hi.astype(F32)).astype(BF16)
    acc = _dot(a_hi, w_hi) + _dot(a_lo, w_hi) + _dot(a_hi, w_lo)
    o_ref[0] = acc + b_ref[0]


def _modulation(cs, mod_w, mod_b):
    depth, d, n3 = mod_w.shape
    rows = cs.shape[0]
    tn = 768 if n3 % 768 == 0 else n3
    return pl.pallas_call(
        _mod_kernel,
        grid=(depth, n3 // tn),
        in_specs=[
            pl.BlockSpec((rows, d), lambda i, n: (0, 0)),
            pl.BlockSpec((1, d, tn), lambda i, n: (i, 0, n)),
            pl.BlockSpec((1, 1, tn), lambda i, n: (i, 0, n)),
        ],
        out_specs=pl.BlockSpec((1, rows, tn), lambda i, n: (i, 0, n)),
        out_shape=jax.ShapeDtypeStruct((depth, rows, n3), F32),
        compiler_params=_params(("parallel", "parallel")),
        name="modulation",
    )(cs, mod_w, mod_b.reshape(depth, 1, n3))


def _norm_mod(xb, g, scale, shift):
    ms = jnp.mean(xb * xb, axis=-1, keepdims=True)
    return (xb * lax.rsqrt(ms + EPS) * g) * (1.0 + scale) + shift


def _slab_pitch(nb):
    return nb + 8


def _lru_in_kernel(x_ref, sh_ref, sc_ref, g_ref, w_ref, *rest, nb, tt, sub, d, w, nch):
    u_ref, gs_ref = rest[-2 - sub:-sub]
    slabs = rest[-sub:]
    m = tt * nb
    pitch = _slab_pitch(nb)
    g = g_ref[...]
    for k in range(sub):
        slab_ref = slabs[k]
        for b in range(nb):
            h = _norm_mod(x_ref[b, k * tt:(k + 1) * tt, :], g, sc_ref[b], sh_ref[b])
            for s in range(d // LANES):
                slab_ref[s, pl.ds(b, tt, stride=pitch), :] = h[:, s * LANES:(s + 1) * LANES]
        hfull = jnp.concatenate(
            [jnp.concatenate([slab_ref[s, t * pitch:t * pitch + nb, :] for t in range(tt)], axis=0)
             for s in range(d // LANES)], axis=1).astype(BF16)
        rows = slice(k * m, (k + 1) * m)
        for n in range(0, 2 * w, nch):
            res = _dot(hfull, w_ref[:, n:n + nch])
            if n < w:
                u_ref[rows, n:n + nch] = res.astype(BF16)
            else:
                gs_ref[rows, n - w:n - w + nch] = _silu(res).astype(BF16)


def _lru_in(xseq, shift, scale, g, w_in, *, total_len, row_off, into=None):
    nb, l, d = xseq.shape
    w = w_in.shape[1] // 2
    tt, sub = LRU_TT, LRU_SUB
    step = sub * tt
    assert l % step == 0 and row_off % step == 0
    m = tt * nb
    off = row_off // step
    nch = min(512, w)
    kern = functools.partial(_lru_in_kernel, nb=nb, tt=tt, sub=sub, d=d, w=w, nch=nch)
    bmod = pl.BlockSpec((nb, 1, d), lambda j: (0, 0, 0))
    args = [xseq, shift, scale, g, w_in]
    in_specs = [pl.BlockSpec((nb, step, d), lambda j: (0, j, 0)), bmod, bmod, pl.BlockSpec((1, d), lambda j: (0, 0)),
                _resident((d, 2 * w))]
    aliases = {}
    if into is not None:
        aliases = {len(args): 0, len(args) + 1: 1}
        args += list(into)
        in_specs += [pl.BlockSpec(memory_space=pl.ANY)] * 2
    out = jax.ShapeDtypeStruct((total_len * nb, w), BF16)
    tiles = pl.BlockSpec((sub * m, w), lambda j: (j + off, 0))
    return pl.pallas_call(
        kern,
        grid=(l // step,),
        in_specs=in_specs,
        out_specs=[tiles, tiles],
        out_shape=[out, out],
        scratch_shapes=[pltpu.VMEM((d // LANES, tt * _slab_pitch(nb), LANES), F32)] * sub,
        input_output_aliases=aliases,
        compiler_params=_params(("parallel",)),
        name="lru_in",
    )(*args)


def _scan_chunk(j, nc, ntot, reverse):
    if not reverse:
        return j
    return jnp.where(j < nc, nc - 1 - j, ntot - 1 - j + nc)


def _lru_scan_kernel(prev_ref, cur_ref, next_ref, cw_ref, cb_ref, gw_ref, lam_ref, hs_ref,
                     a_ref, b_ref, h_ref, *, nc, ntot, nb, tc, wb, reverse):
    j = pl.program_id(1)
    c = _scan_chunk(j, nc, ntot, reverse)
    seq_start = (c == 0) | (c == nc)
    seq_end = (c == nc - 1) | (c == ntot - 1)
    m = tc * nb

    @pl.when(j == 0)
    def _():
        h_ref[...] = jnp.zeros_like(h_ref)

    prev = jnp.where(seq_start, 0.0, prev_ref[...].astype(F32))
    nxt = jnp.where(seq_end, 0.0, next_ref[0:nb, :].astype(F32))
    ext = jnp.concatenate([prev, cur_ref[...].astype(F32), nxt], axis=0)
    cw = cw_ref[...]
    xc = cb_ref[...] + cw[0:1] * ext[0:m]
    for k in range(1, 4):
        xc = xc + cw[k:k + 1] * ext[k * nb:k * nb + m]

    lam = lam_ref[...]
    half_c = (0.5 * LRU_C) * (jnp.maximum(-lam, 0.0) + jnp.log1p(jnp.exp(-jnp.abs(lam))))
    nslab = wb // LANES
    ones = jnp.ones((m, LANES), BF16)
    for s in range(nslab):
        xs = xc[:, s * LANES:(s + 1) * LANES]
        th = jnp.tanh(_dot(jnp.concatenate([xs.astype(BF16), ones], axis=1), gw_ref[s]))
        hc = half_c[:, s * LANES:(s + 1) * LANES]
        nla = th[:, :LANES] * hc + hc
        a = jnp.exp2(nla * (-math.log2(math.e)))
        a_ref[s] = a
        y = jnp.tanh(nla) * (a * a + 1.0)
        sq = jnp.where(y > 0.0, y * lax.rsqrt(y), 0.0)
        b_ref[s] = sq * ((0.5 * th[:, LANES:] + 0.5) * xs)

    def step(i, hs):
        t = (tc - 1 - i) if reverse else i
        rows = pl.ds(pl.multiple_of(t * nb, nb), nb)
        out = []
        for s in range(nslab):
            h = a_ref[s, rows, :] * hs[s] + b_ref[s, rows, :]
            hs_ref[rows, s * LANES:(s + 1) * LANES] = h.astype(BF16)
            out.append(h)
        return tuple(out)

    hs = lax.fori_loop(0, tc, step, tuple(h_ref[s] for s in range(nslab)), unroll=4)
    for s in range(nslab):
        h_ref[s] = hs[s]


def _lru_scan(u_tm, conv_w, conv_b, gw, lam, *, nb, c, reverse):
    rows, w = u_tm.shape
    ltot = rows // nb
    tc = math.gcd(c, 64)
    wb = min(512, w)
    nc, ntot = c // tc, ltot // tc
    m = tc * nb
    hb = tc // 2
    last_halo = ltot // 2 - 1
    chunk = functools.partial(_scan_chunk, nc=nc, ntot=ntot, reverse=reverse)
    kern = functools.partial(_lru_scan_kernel, nc=nc, ntot=ntot, nb=nb, tc=tc, wb=wb, reverse=reverse)
    nslab = wb // LANES
    return pl.pallas_call(
        kern,
        grid=(w // wb, ntot),
        in_specs=[
            pl.BlockSpec((2 * nb, wb), lambda i, j: (jnp.maximum(chunk(j) * hb - 1, 0), i)),
            pl.BlockSpec((m, wb), lambda i, j: (chunk(j), i)),
            pl.BlockSpec((2 * nb, wb), lambda i, j: (jnp.minimum((chunk(j) + 1) * hb, last_halo), i)),
            pl.BlockSpec((4, wb), lambda i, j: (0, i)),
            pl.BlockSpec((1, wb), lambda i, j: (0, i)),
            pl.BlockSpec((nslab, 2 * LANES, 2 * LANES), lambda i, j: (i, 0, 0)),
            pl.BlockSpec((1, wb), lambda i, j: (0, i)),
        ],
        out_specs=pl.BlockSpec((m, wb), lambda i, j: (chunk(j), i)),
        out_shape=jax.ShapeDtypeStruct((rows, w), BF16),
        scratch_shapes=[
            pltpu.VMEM((nslab, m, LANES), F32),
            pltpu.VMEM((nslab, m, LANES), F32),
            pltpu.VMEM((nslab, nb, LANES), F32),
        ],
        compiler_params=_params(("parallel", "arbitrary")),
        name="lru_scan_bwd" if reverse else "lru_scan_fwd",
    )(u_tm, u_tm, u_tm, conv_w, conv_b, gw, lam)


def _lru_out_kernel(hf_ref, hb_ref, gs_ref, w_ref, x_ref, gate_ref, g_ref, sh_ref, sc_ref, x1_ref, h1_ref,
                    *slabs, nb, tt, d, nch):
    m = tt * nb
    per = nch // LANES
    pitch = _slab_pitch(nb)
    g = g_ref[...]
    for k, slab_ref in enumerate(slabs):
        rows = slice(k * m, (k + 1) * m)
        z = ((hf_ref[rows, :].astype(F32) + hb_ref[rows, :].astype(F32)) * gs_ref[rows, :].astype(F32)).astype(BF16)
        for c in range(d // nch):
            y = _dot(z, w_ref[:, c * nch:(c + 1) * nch])
            for j in range(per):
                for t in range(tt):
                    slab_ref[c * per + j, t * pitch:t * pitch + nb, :] = y[t * nb:(t + 1) * nb, j * LANES:(j + 1) * LANES]
        t0 = k * tt
        for b in range(nb):
            yb = jnp.concatenate([slab_ref[s, pl.ds(b, tt, stride=pitch), :] for s in range(d // LANES)], axis=1)
            x1 = x_ref[b, t0:t0 + tt, :] + gate_ref[b] * yb
            x1_ref[b, t0:t0 + tt, :] = x1
            h1_ref[b, t0:t0 + tt, :] = _norm_mod(x1, g, sc_ref[b], sh_ref[b]).astype(BF16)


def _lru_out(hs_f, hs_b, gs, w_out, xres, gate, g1, shift1, scale1, *, row_off):
    nb, l, d = xres.shape
    w = w_out.shape[0]
    tt, sub = LRU_TT, LRU_SUB
    step = sub * tt
    assert l % step == 0 and row_off % step == 0
    m = tt * nb
    off = row_off // step
    kern = functools.partial(_lru_out_kernel, nb=nb, tt=tt, d=d, nch=min(512, d))
    bmod = pl.BlockSpec((nb, 1, d), lambda j: (0, 0, 0))
    act = pl.BlockSpec((sub * m, w), lambda j: (j + off, 0))
    res = pl.BlockSpec((nb, step, d), lambda j: (0, j, 0))
    return pl.pallas_call(
        kern,
        grid=(l // step,),
        in_specs=[act, act, act, _resident((w, d)), res, bmod, pl.BlockSpec((1, d), lambda j: (0, 0)), bmod, bmod],
        out_specs=[res, res],
        out_shape=[jax.ShapeDtypeStruct((nb, l, d), F32), jax.ShapeDtypeStruct((nb, l, d), BF16)],
        scratch_shapes=[pltpu.VMEM((d // LANES, tt * _slab_pitch(nb), LANES), F32)] * sub,
        compiler_params=_params(("parallel",)),
        name="lru_out",
    )(hs_f, hs_b, gs, w_out, xres, gate, g1, shift1, scale1)


def _att_proj_kernel(*refs, mode, rope, nheads, aliased):
    refs = list(refs)
    h_ref, w_ref = refs[0], refs[1]
    pos = 2
    if mode in ("q", "k"):
        gain_ref, gmat_ref = refs[pos], refs[pos + 1]
        pos += 2
        if rope:
            cos_ref, sa_ref, sb_ref = refs[pos:pos + 3]
            pos += 3
    if aliased:
        pos += 1
    outs = refs[pos:]
    if mode == "v":
        res_t = lax.dot_general(w_ref[...], h_ref[0], _NT, preferred_element_type=F32)
        ones = jnp.ones((V_ROWS - LANES, res_t.shape[1]), BF16)
        for h in range(nheads):
            outs[0][0, h, 0:LANES, :] = res_t[h * LANES:(h + 1) * LANES, :].astype(BF16)
            outs[0][0, h, LANES:V_ROWS, :] = ones
        return
    res_ref = outs.pop()

    @pl.when(pl.program_id(0) == 0)
    def _():
        res_ref[...] = jnp.zeros_like(res_ref)

    for p in range(res_ref.shape[1] // MXU_DIM):
        r = res_ref[:, p * MXU_DIM:(p + 1) * MXU_DIM]
        if mode == "g":
            outs[0][0, :, p * MXU_DIM:(p + 1) * MXU_DIM] = _silu(r).astype(BF16)
            continue
        ms = _dot((r * r).astype(BF16), gmat_ref[...])
        rn = r * lax.rsqrt(ms + EPS) * gain_ref[:, p * MXU_DIM:(p + 1) * MXU_DIM]
        for hh in range(MXU_DIM // LANES):
            xh = rn[:, hh * LANES:(hh + 1) * LANES]
            if rope:
                xh = (xh * cos_ref[...] + pltpu.roll(xh, LANES - ROPE_FREQS, 1) * sa_ref[...]
                      + pltpu.roll(xh, ROPE_FREQS, 1) * sb_ref[...])
            outs[0][0, p * (MXU_DIM // LANES) + hh] = xh.astype(BF16)
    res_ref[...] = _dot(h_ref[0], w_ref[...])


def _att_proj(h1, w, *, mode, gain=None, gmat=None, rope_tabs=None, out_len=None, out_off=0, into=None):
    nb, l, d = h1.shape
    n = w.shape[0] if mode == "v" else w.shape[1]
    tr = min(512, l)
    nheads = n // LANES
    out_len = l if out_len is None else out_len
    off = out_off // tr
    rope = rope_tabs is not None
    nt = l // tr
    ntiles = nb * nt
    lag = 0 if mode == "v" else 1

    def cur(j):
        t = jnp.minimum(j, ntiles - 1)
        return t // nt, t % nt

    def done(j):
        t = jnp.maximum(j - lag, 0)
        return t // nt, t % nt

    const = lambda j: (0, 0)
    args = [h1, w]
    in_specs = [pl.BlockSpec((1, tr, d), lambda j: (*cur(j), 0)), _resident(w.shape)]
    scratch = [pltpu.VMEM((tr, n), F32)] if lag else []
    if mode in ("q", "k"):
        args += [gain, gmat]
        in_specs += [pl.BlockSpec((1, n), const), pl.BlockSpec((MXU_DIM, MXU_DIM), const)]
        if rope:
            args += list(rope_tabs)
            in_specs += [pl.BlockSpec((tr, LANES), lambda j: (done(j)[1], 0))] * 3
    aliases = {}
    if into is not None:
        aliases = {len(args): 0}
        args.append(into)
        in_specs.append(pl.BlockSpec(memory_space=pl.ANY))
    if mode == "g":
        out_specs = [pl.BlockSpec((1, tr, n), lambda j: (*done(j), 0))]
        out_shape = [jax.ShapeDtypeStruct((nb, l, n), BF16)]
    elif mode == "v":
        out_specs = [pl.BlockSpec((1, nheads, V_ROWS, tr), lambda j: (done(j)[0], 0, 0, done(j)[1] + off))]
        out_shape = [jax.ShapeDtypeStruct((nb, nheads, V_ROWS, out_len), BF16)]
    else:
        hspec = pl.BlockSpec((1, nheads, tr, LANES), lambda j: (done(j)[0], 0, done(j)[1] + off, 0))
        hshape = jax.ShapeDtypeStruct((nb, nheads, out_len, LANES), BF16)
        out_specs, out_shape = [hspec], [hshape]
    kern = functools.partial(_att_proj_kernel, mode=mode, rope=rope, nheads=nheads, aliased=into is not None)
    return pl.pallas_call(
        kern,
        grid=(ntiles + lag,),
        in_specs=in_specs,
        out_specs=out_specs,
        out_shape=out_shape,
        scratch_shapes=scratch,
        input_output_aliases=aliases,
        compiler_params=_params(("arbitrary",)),
        name="att_proj_" + mode + ("_rope" if rope else ""),
    )(*args)


def _attn_kernel(q_ref, k_ref, vt_ref, lv_ref, sg_ref, o_ref, st_ref, m_ref, p_ref, *, lam_init, tq):
    @pl.when(pl.program_id(0) == 0)
    def _():
        st_ref[...] = jnp.zeros_like(st_ref)
        m_ref[...] = jnp.zeros_like(m_ref)
        p_ref[...] = jnp.ones_like(p_ref)

    lv = lv_ref[...]
    lam = (jnp.exp(jnp.sum(lv[0:1] * lv[1:2], axis=-1, keepdims=True))
           - jnp.exp(jnp.sum(lv[2:3] * lv[3:4], axis=-1, keepdims=True)) + lam_init)
    ot = _dot(vt_ref[0, 0], p_ref[...])
    on = ot[0:LANES] * (1.0 / ot[LANES:LANES + 1])
    od = on[:, 0:tq] - lam * on[:, tq:2 * tq]
    ms = jnp.mean(od * od, axis=0, keepdims=True)
    o_ref[0, 0] = (od * lax.rsqrt(ms + EPS) * sg_ref[...]).T.astype(BF16)

    p_ref[...] = jnp.exp2((st_ref[...] - m_ref[...]).astype(BF16))

    qh = q_ref[0, 0]
    first = lax.broadcasted_iota(jnp.int32, qh.shape, 1) < HEAD_DIM
    zero = jnp.zeros_like(qh)
    q = jnp.concatenate([jnp.where(first, qh, zero), jnp.where(first, zero, qh)], axis=0)
    st = lax.dot_general(k_ref[0, 0], q, _NT, preferred_element_type=F32)
    st_ref[...] = st
    m_ref[...] = jnp.max(st, axis=0, keepdims=True)


def _attention(qh, kh, vt, lam_vecs, subln_t, *, lam_init):
    nb, nh, s, _ = qh.shape
    ltot = kh.shape[2]
    tq = subln_t.shape[1]
    nq = s // tq
    ntiles = nb * nh * nq

    def tile(t):
        return t // (nh * nq), (t // nq) % nh, t % nq

    def q_map(i):
        b, h, qi = tile(jnp.minimum(i, ntiles - 1))
        return b, h, qi, 0

    def k_map(i):
        b, h, _ = tile(jnp.minimum(i, ntiles - 1))
        return b, h, 0, 0

    def v_map(i):
        b, h, _ = tile(jnp.maximum(i - 2, 0))
        return b, h, 0, 0

    def o_map(i):
        b, h, qi = tile(jnp.maximum(i - 2, 0))
        return b, h, qi, 0

    qspec = pl.BlockSpec((1, 1, tq, LANES), q_map)
    return pl.pallas_call(
        functools.partial(_attn_kernel, lam_init=lam_init, tq=tq),
        grid=(ntiles + 2,),
        in_specs=[qspec,
                  pl.BlockSpec((1, 1, ltot, LANES), k_map),
                  pl.BlockSpec((1, 1, V_ROWS, ltot), v_map),
                  pl.BlockSpec((4, HEAD_DIM), lambda i: (0, 0)),
                  pl.BlockSpec((LANES, tq), lambda i: (0, 0))],
        out_specs=pl.BlockSpec((1, 1, tq, LANES), o_map),
        out_shape=jax.ShapeDtypeStruct((nb, nh, s, LANES), BF16),
        scratch_shapes=[pltpu.VMEM((ltot, 2 * tq), F32), pltpu.VMEM((1, 2 * tq), F32),
                        pltpu.VMEM((ltot, 2 * tq), BF16)],
        compiler_params=_params(("arbitrary",)),
        name="diff_attention",
    )(qh, kh, vt, lam_vecs, subln_t)


def _att_out_kernel(o_ref, gs_ref, w_ref, x_ref, gate_ref, out_ref, *, nheads):
    o = jnp.concatenate([o_ref[0, h] for h in range(nheads)], axis=1)
    z = (o.astype(F32) * gs_ref[0].astype(F32)).astype(BF16)
    out_ref[0] = x_ref[0] + gate_ref[0] * _dot(z, w_ref[...])


def _att_out(o_heads, gs, w_out, x1, gate):
    nb, nh, s, _ = o_heads.shape
    d = w_out.shape[1]
    wd = w_out.shape[0]
    tr = min(512, s)
    return pl.pallas_call(
        functools.partial(_att_out_kernel, nheads=nh),
        grid=(nb, s // tr),
        in_specs=[
            pl.BlockSpec((1, nh, tr, LANES), lambda b, i: (b, 0, i, 0)),
            pl.BlockSpec((1, tr, wd), lambda b, i: (b, i, 0)),
            _resident((wd, d)),
            pl.BlockSpec((1, tr, d), lambda b, i: (b, i, 0)),
            pl.BlockSpec((1, 1, d), lambda b, i: (b, 0, 0)),
        ],
        out_specs=pl.BlockSpec((1, tr, d), lambda b, i: (b, i, 0)),
        out_shape=jax.ShapeDtypeStruct((nb, s, d), F32),
        compiler_params=_params(("parallel", "parallel")),
        name="att_out",
    )(o_heads, gs, w_out, x1, gate)


def _gate_weights(gw, gb):
    w = 0.5 * jnp.concatenate([gw[0], gw[1]], axis=-1)
    b = 0.5 * jnp.concatenate([gb[0], gb[1]], axis=-1)
    b_hi = b.astype(BF16)
    b_lo = (b - b_hi.astype(F32)).astype(BF16)
    pad = jnp.zeros((w.shape[0], LANES - 2, 2 * LANES), BF16)
    return jnp.concatenate([w.astype(BF16), b_hi[:, None, :], b_lo[:, None, :], pad], axis=1)


def _rope_tables(s):
    t = jnp.arange(s)
    pos = jnp.stack([(t // GRID_W).astype(F32), (t % GRID_W).astype(F32)], axis=1)
    inv = ROPE_BASE ** (-jnp.arange(ROPE_FREQS, dtype=F32) / ROPE_FREQS)
    ang = pos[:, :, None] * inv
    lane = jnp.arange(LANES)
    dd = lane % HEAD_DIM
    axis, half, f = dd // (2 * ROPE_FREQS), (dd % (2 * ROPE_FREQS)) // ROPE_FREQS, dd % ROPE_FREQS
    cos = jnp.cos(ang)[:, axis, f]
    sin = jnp.sin(ang)[:, axis, f]
    return cos, jnp.where(half == 0, -sin, 0.0), jnp.where(half == 1, sin, 0.0)


def kernel(x, c, ctx, c_ctx, mod_w, mod_b, norm_g, lru_w_in, lru_conv_w, lru_conv_b, lru_gate_w, lru_gate_b,
           lru_lambda, lru_w_out, att_w_in, att_q_norm, att_k_norm, att_lambda, att_subln, att_w_out):
    nb, s, d = x.shape
    clen = ctx.shape[1]
    aw = att_w_out.shape[1]
    assert lru_gate_w.shape[-1] == LANES and d % MXU_DIM == 0 and nb % 8 == 0

    rows = ((nb + 1 + 7) // 8) * 8
    cs = jnp.zeros((rows, d), F32).at[:nb].set(c).at[nb].set(c_ctx)
    mod = _modulation(cs, mod_w, mod_b)

    def split(i):
        lat = [mod[i, :nb, k * d:(k + 1) * d].reshape(nb, 1, d) for k in range(3)]
        cx = [jnp.broadcast_to(mod[i, nb:nb + 1, k * d:(k + 1) * d], (nb, d)).reshape(nb, 1, d) for k in range(3)]
        return lat, cx

    (shift0, scale0, gate0), (cshift0, cscale0, cgate0) = split(0)
    (shift1, scale1, gate1), (cshift1, cscale1, _) = split(1)

    w_in0 = lru_w_in[0].astype(BF16)
    ltot = s + clen
    tm = _lru_in(ctx, cshift0, cscale0, norm_g[0:1], w_in0, total_len=ltot, row_off=0)
    u_tm, gs_tm = _lru_in(x, shift0, scale0, norm_g[0:1], w_in0, total_len=ltot, row_off=clen, into=tm)
    hs = []
    for dr in range(2):
        hs.append(_lru_scan(u_tm, lru_conv_w[0], lru_conv_b[0:1], _gate_weights(lru_gate_w[0, dr], lru_gate_b[0, dr]),
                            lru_lambda[0, dr:dr + 1], nb=nb, c=clen, reverse=bool(dr)))
    w_out0 = lru_w_out[0].astype(BF16)
    g1 = norm_g[1:2]
    _, h1_ctx = _lru_out(hs[0], hs[1], gs_tm, w_out0, ctx, cgate0, g1, cshift1, cscale1, row_off=0)
    x1, h1_lat = _lru_out(hs[0], hs[1], gs_tm, w_out0, x, gate0, g1, shift1, scale1, row_off=clen)

    lam_init = 0.8 - 0.6 * math.exp(-0.3 * 1)
    wq, wk, wv, wg = (att_w_in[0][:, k * aw:(k + 1) * aw].astype(BF16) for k in range(4))
    gmat = jnp.kron(jnp.eye(MXU_DIM // HEAD_DIM, dtype=F32),
                    jnp.full((HEAD_DIM, HEAD_DIM), 1.0 / HEAD_DIM, F32)).astype(BF16)
    reps = aw // HEAD_DIM
    q_gain = jnp.tile(att_q_norm[0] * (HEAD_DIM ** -0.5 * math.log2(math.e)), reps)[None, :]
    k_gain = jnp.tile(att_k_norm[0], reps)[None, :]
    tabs = _rope_tables(s)
    qh, = _att_proj(h1_lat, wq, mode="q", gain=q_gain, gmat=gmat, rope_tabs=tabs)
    kh, = _att_proj(h1_lat, wk, mode="k", gain=k_gain, gmat=gmat, rope_tabs=tabs, out_len=ltot)
    kh, = _att_proj(h1_ctx, wk, mode="k", gain=k_gain, gmat=gmat, out_len=ltot, out_off=s, into=kh)
    vt, = _att_proj(h1_lat, wv.T, mode="v", out_len=ltot)
    vt, = _att_proj(h1_ctx, wv.T, mode="v", out_len=ltot, out_off=s, into=vt)
    gsa, = _att_proj(h1_lat, wg, mode="g")
    subln_t = jnp.broadcast_to((att_subln[0] * (1.0 - lam_init))[:, None], (LANES, min(ATT_TQ, s)))
    o_heads = _attention(qh, kh, vt, att_lambda[0], subln_t, lam_init=lam_init)
    return _att_out(o_heads, gsa, att_w_out[0].astype(BF16), x1, gate1)
```

```python
import functools
import math

import jax
import jax.numpy as jnp
from jax import lax
from jax.experimental import pallas as pl
from jax.experimental.pallas import tpu as pltpu

F32 = jnp.float32
BF16 = jnp.bfloat16

EPS = 1e-6
LRU_C = 8.0
GRID_W = 64
HEAD_DIM = 64
ROPE_FREQS = HEAD_DIM // 4
ROPE_BASE = 10000.0
LANES = 128
MXU_DIM = 256
VMEM_LIMIT = 56 * 1024 * 1024
V_ROWS = LANES + 16
ATT_TQ = 1024
LRU_TT = 32
LRU_SUB = 1
_NT = (((1,), (1,)), ((), ()))


def _params(sem, vmem=VMEM_LIMIT):
    return pltpu.CompilerParams(dimension_semantics=sem, vmem_limit_bytes=vmem)


def _sigmoid(x):
    return 1.0 / (1.0 + jnp.exp(-x))


def _silu(x):
    return x * _sigmoid(x)


def _dot(a, b):
    return jnp.dot(a, b, preferred_element_type=F32)


def _resident(shape):
    nd = len(shape)
    return pl.BlockSpec(shape, lambda *_: (0,) * nd, pipeline_mode=pl.Buffered(1))


def _mod_kernel(cs_ref, w_ref, b_ref, o_ref):
    a = _silu(cs_ref[...])
    w = w_ref[0]
    a_hi = a.astype(BF16)
    a_lo = (a - a_hi.astype(F32)).astype(BF16)
    w_hi = w.astype(BF16)
    w_lo = (w - w_hi.astype(F32)).astype(BF16)
    acc = _dot(a_hi, w_hi) + _dot(a_lo, w_hi) + _dot(a_hi, w_lo)
    o_ref[0] = acc + b_ref[0]


def _modulation(cs, mod_w, mod_b):
    depth, d, n3 = mod_w.shape
    rows = cs.shape[0]
    tn = 768 if n3 % 768 == 0 else n3
    return pl.pallas_call(
        _mod_kernel,
        grid=(depth, n3 // tn),
        in_specs=[
            pl.BlockSpec((rows, d), lambda i, n: (0, 0)),
            pl.BlockSpec((1, d, tn), lambda i, n: (i, 0, n)),
            pl.BlockSpec((1, 1, tn), lambda i, n: (i, 0, n)),
        ],
        out_specs=pl.BlockSpec((1, rows, tn), lambda i, n: (i, 0, n)),
        out_shape=jax.ShapeDtypeStruct((depth, rows, n3), F32),
        compiler_params=_params(("parallel", "parallel")),
        name="modulation",
    )(cs, mod_w, mod_b.reshape(depth, 1, n3))


def _norm_mod(xb, g, scale, shift):
    ms = jnp.mean(xb * xb, axis=-1, keepdims=True)
    return (xb * lax.rsqrt(ms + EPS) * g) * (1.0 + scale) + shift


def _slab_pitch(nb):
    return nb + 8


def _lru_in_kernel(ctx_ref, x_ref, sh_ref, sc_ref, g_ref, w_ref, u_ref, gs_ref, *slabs, nc, nb, tt, d, w, nch):
    is_ctx = pl.program_id(0) < nc
    m = tt * nb
    pitch = _slab_pitch(nb)
    g = g_ref[...]
    for k, slab_ref in enumerate(slabs):
        for b in range(nb):
            xb = jnp.where(is_ctx, ctx_ref[b, k * tt:(k + 1) * tt, :], x_ref[b, k * tt:(k + 1) * tt, :])
            h = _norm_mod(xb, g, sc_ref[0, b:b + 1, :], sh_ref[0, b:b + 1, :])
            for s in range(d // LANES):
                slab_ref[s, pl.ds(b, tt, stride=pitch), :] = h[:, s * LANES:(s + 1) * LANES]
        hfull = jnp.concatenate(
            [jnp.concatenate([slab_ref[s, t * pitch:t * pitch + nb, :] for t in range(tt)], axis=0)
             for s in range(d // LANES)], axis=1).astype(BF16)
        rows = slice(k * m, (k + 1) * m)
        for n in range(0, 2 * w, nch):
            res = _dot(hfull, w_ref[:, n:n + nch])
            if n < w:
                u_ref[rows, n:n + nch] = res.astype(BF16)
            else:
                gs_ref[rows, n - w:n - w + nch] = _silu(res).astype(BF16)


def _lru_in(ctx, x, shift2, scale2, g, w_in):
    nb, c, d = ctx.shape
    s = x.shape[1]
    w = w_in.shape[1] // 2
    tt, sub = LRU_TT, LRU_SUB
    step = sub * tt
    assert c % step == 0 and s % step == 0
    nc = c // step
    m = tt * nb
    nch = min(512, w)
    kern = functools.partial(_lru_in_kernel, nc=nc, nb=nb, tt=tt, d=d, w=w, nch=nch)
    mod = pl.BlockSpec((1, nb, d), lambda j: (jnp.where(j < nc, 0, 1), 0, 0))
    out = jax.ShapeDtypeStruct(((c + s) * nb, w), BF16)
    tiles = pl.BlockSpec((sub * m, w), lambda j: (j, 0))
    return pl.pallas_call(
        kern,
        grid=((c + s) // step,),
        in_specs=[
            pl.BlockSpec((nb, step, d), lambda j: (0, jnp.minimum(j, nc - 1), 0)),
            pl.BlockSpec((nb, step, d), lambda j: (0, jnp.maximum(j - nc, 0), 0)),
            mod, mod, pl.BlockSpec((1, d), lambda j: (0, 0)),
            _resident((d, 2 * w)),
        ],
        out_specs=[tiles, tiles],
        out_shape=[out, out],
        scratch_shapes=[pltpu.VMEM((d // LANES, tt * _slab_pitch(nb), LANES), F32)] * sub,
        compiler_params=_params(("parallel",)),
        name="lru_in",
    )(ctx, x, shift2, scale2, g, w_in)


def _scan_chunk(j, nc, ntot, reverse):
    if not reverse:
        return j
    return jnp.where(j < nc, nc - 1 - j, ntot - 1 - j + nc)


def _lru_scan_kernel(prev_ref, cur_ref, next_ref, cw_ref, cb_ref, gw_ref, lam_ref, hs_ref,
                     a_ref, b_ref, h_ref, *, nc, ntot, nb, tc, wb, reverse):
    j = pl.program_id(1)
    c = _scan_chunk(j, nc, ntot, reverse)
    seq_start = (c == 0) | (c == nc)
    seq_end = (c == nc - 1) | (c == ntot - 1)
    m = tc * nb

    @pl.when(j == 0)
    def _():
        h_ref[...] = jnp.zeros_like(h_ref)

    prev = jnp.where(seq_start, 0.0, prev_ref[...].astype(F32))
    nxt = jnp.where(seq_end, 0.0, next_ref[0:nb, :].astype(F32))
    ext = jnp.concatenate([prev, cur_ref[...].astype(F32), nxt], axis=0)
    cw = cw_ref[...]
    xc = cb_ref[...] + cw[0:1] * ext[0:m]
    for k in range(1, 4):
        xc = xc + cw[k:k + 1] * ext[k * nb:k * nb + m]

    lam = lam_ref[...]
    half_c = (0.5 * LRU_C) * (jnp.maximum(-lam, 0.0) + jnp.log1p(jnp.exp(-jnp.abs(lam))))
    nslab = wb // LANES
    ones = jnp.ones((m, LANES), BF16)
    for s in range(nslab):
        xs = xc[:, s * LANES:(s + 1) * LANES]
        th = jnp.tanh(_dot(jnp.concatenate([xs.astype(BF16), ones], axis=1), gw_ref[s]))
        hc = half_c[:, s * LANES:(s + 1) * LANES]
        nla = th[:, :LANES] * hc + hc
        a = jnp.exp2(nla * (-math.log2(math.e)))
        a_ref[s] = a
        y = jnp.tanh(nla) * (a * a + 1.0)
        sq = jnp.where(y > 0.0, y * lax.rsqrt(y), 0.0)
        b_ref[s] = sq * ((0.5 * th[:, LANES:] + 0.5) * xs)

    def step(i, hs):
        t = (tc - 1 - i) if reverse else i
        rows = pl.ds(pl.multiple_of(t * nb, nb), nb)
        out = []
        for s in range(nslab):
            h = a_ref[s, rows, :] * hs[s] + b_ref[s, rows, :]
            hs_ref[rows, s * LANES:(s + 1) * LANES] = h.astype(BF16)
            out.append(h)
        return tuple(out)

    hs = lax.fori_loop(0, tc, step, tuple(h_ref[s] for s in range(nslab)), unroll=4)
    for s in range(nslab):
        h_ref[s] = hs[s]


def _lru_scan(u_tm, conv_w, conv_b, gw, lam, *, nb, c, reverse):
    rows, w = u_tm.shape
    ltot = rows // nb
    tc = math.gcd(c, 64)
    wb = min(512, w)
    nc, ntot = c // tc, ltot // tc
    m = tc * nb
    hb = tc // 2
    last_halo = ltot // 2 - 1
    chunk = functools.partial(_scan_chunk, nc=nc, ntot=ntot, reverse=reverse)
    kern = functools.partial(_lru_scan_kernel, nc=nc, ntot=ntot, nb=nb, tc=tc, wb=wb, reverse=reverse)
    nslab = wb // LANES
    return pl.pallas_call(
        kern,
        grid=(w // wb, ntot),
        in_specs=[
            pl.BlockSpec((2 * nb, wb), lambda i, j: (jnp.maximum(chunk(j) * hb - 1, 0), i)),
            pl.BlockSpec((m, wb), lambda i, j: (chunk(j), i)),
            pl.BlockSpec((2 * nb, wb), lambda i, j: (jnp.minimum((chunk(j) + 1) * hb, last_halo), i)),
            pl.BlockSpec((4, wb), lambda i, j: (0, i)),
            pl.BlockSpec((1, wb), lambda i, j: (0, i)),
            pl.BlockSpec((nslab, 2 * LANES, 2 * LANES), lambda i, j: (i, 0, 0)),
            pl.BlockSpec((1, wb), lambda i, j: (0, i)),
        ],
        out_specs=pl.BlockSpec((m, wb), lambda i, j: (chunk(j), i)),
        out_shape=jax.ShapeDtypeStruct((rows, w), BF16),
        scratch_shapes=[
            pltpu.VMEM((nslab, m, LANES), F32),
            pltpu.VMEM((nslab, m, LANES), F32),
            pltpu.VMEM((nslab, nb, LANES), F32),
        ],
        compiler_params=_params(("parallel", "arbitrary")),
        name="lru_scan_bwd" if reverse else "lru_scan_fwd",
    )(u_tm, u_tm, u_tm, conv_w, conv_b, gw, lam)


def _lru_out_kernel(hf_ref, hb_ref, gs_ref, w_ref, x_ref, gate_ref, g_ref, sh_ref, sc_ref, x1_ref, h1_ref,
                    *slabs, nb, tt, d, nch):
    m = tt * nb
    per = nch // LANES
    pitch = _slab_pitch(nb)
    g = g_ref[...]
    for k, slab_ref in enumerate(slabs):
        rows = slice(k * m, (k + 1) * m)
        z = ((hf_ref[rows, :].astype(F32) + hb_ref[rows, :].astype(F32)) * gs_ref[rows, :].astype(F32)).astype(BF16)
        for c in range(d // nch):
            y = _dot(z, w_ref[:, c * nch:(c + 1) * nch])
            for j in range(per):
                for t in range(tt):
                    slab_ref[c * per + j, t * pitch:t * pitch + nb, :] = y[t * nb:(t + 1) * nb, j * LANES:(j + 1) * LANES]
        t0 = k * tt
        for b in range(nb):
            yb = jnp.concatenate([slab_ref[s, pl.ds(b, tt, stride=pitch), :] for s in range(d // LANES)], axis=1)
            x1 = x_ref[b, t0:t0 + tt, :] + gate_ref[b] * yb
            x1_ref[b, t0:t0 + tt, :] = x1
            h1_ref[b, t0:t0 + tt, :] = _norm_mod(x1, g, sc_ref[b], sh_ref[b]).astype(BF16)


def _lru_out(hs_f, hs_b, gs, w_out, xres, gate, g1, shift1, scale1, *, row_off):
    nb, l, d = xres.shape
    w = w_out.shape[0]
    tt, sub = LRU_TT, LRU_SUB
    step = sub * tt
    assert l % step == 0 and row_off % step == 0
    m = tt * nb
    off = row_off // step
    kern = functools.partial(_lru_out_kernel, nb=nb, tt=tt, d=d, nch=min(512, d))
    bmod = pl.BlockSpec((nb, 1, d), lambda j: (0, 0, 0))
    act = pl.BlockSpec((sub * m, w), lambda j: (j + off, 0))
    res = pl.BlockSpec((nb, step, d), lambda j: (0, j, 0))
    return pl.pallas_call(
        kern,
        grid=(l // step,),
        in_specs=[act, act, act, _resident((w, d)), res, bmod, pl.BlockSpec((1, d), lambda j: (0, 0)), bmod, bmod],
        out_specs=[res, res],
        out_shape=[jax.ShapeDtypeStruct((nb, l, d), F32), jax.ShapeDtypeStruct((nb, l, d), BF16)],
        scratch_shapes=[pltpu.VMEM((d // LANES, tt * _slab_pitch(nb), LANES), F32)] * sub,
        compiler_params=_params(("parallel",)),
        name="lru_out",
    )(hs_f, hs_b, gs, w_out, xres, gate, g1, shift1, scale1)


def _att_proj_kernel(*refs, mode, rope, nheads):
    refs = list(refs)
    h_ref, w_ref = refs[0], refs[1]
    pos = 2
    if mode in ("q", "k"):
        gain_ref, gmat_ref = refs[pos], refs[pos + 1]
        pos += 2
        if rope:
            cos_ref, sa_ref, sb_ref = refs[pos:pos + 3]
            pos += 3
    outs = refs[pos:]
    if mode == "v":
        res_t = lax.dot_general(w_ref[...], h_ref[0], _NT, preferred_element_type=F32)
        ones = jnp.ones((V_ROWS - LANES, res_t.shape[1]), BF16)
        for h in range(nheads):
            outs[0][0, h, 0:LANES, :] = res_t[h * LANES:(h + 1) * LANES, :].astype(BF16)
            outs[0][0, h, LANES:V_ROWS, :] = ones
        return
    res_ref = outs.pop()

    @pl.when(pl.program_id(0) == 0)
    def _():
        res_ref[...] = jnp.zeros_like(res_ref)

    for p in range(res_ref.shape[1] // MXU_DIM):
        r = res_ref[:, p * MXU_DIM:(p + 1) * MXU_DIM]
        if mode == "g":
            outs[0][0, :, p * MXU_DIM:(p + 1) * MXU_DIM] = _silu(r).astype(BF16)
            continue
        ms = _dot((r * r).astype(BF16), gmat_ref[...])
        rn = r * lax.rsqrt(ms + EPS) * gain_ref[:, p * MXU_DIM:(p + 1) * MXU_DIM]
        for hh in range(MXU_DIM // LANES):
            xh = rn[:, hh * LANES:(hh + 1) * LANES]
            if rope:
                xh = (xh * cos_ref[...] + pltpu.roll(xh, LANES - ROPE_FREQS, 1) * sa_ref[...]
                      + pltpu.roll(xh, ROPE_FREQS, 1) * sb_ref[...])
            outs[0][0, p * (MXU_DIM // LANES) + hh] = xh.astype(BF16)
    res_ref[...] = _dot(h_ref[0], w_ref[...])


def _att_proj(h1, w, *, mode, gain=None, gmat=None, rope_tabs=None):
    nb, l, d = h1.shape
    n = w.shape[0] if mode == "v" else w.shape[1]
    tr = min(512, l)
    nheads = n // LANES
    rope = rope_tabs is not None
    nt = l // tr
    ntiles = nb * nt
    lag = 0 if mode == "v" else 1

    def cur(j):
        t = jnp.minimum(j, ntiles - 1)
        return t // nt, t % nt

    def done(j):
        t = jnp.maximum(j - lag, 0)
        return t // nt, t % nt

    const = lambda j: (0, 0)
    args = [h1, w]
    in_specs = [pl.BlockSpec((1, tr, d), lambda j: (*cur(j), 0)), _resident(w.shape)]
    scratch = [pltpu.VMEM((tr, n), F32)] if lag else []
    if mode in ("q", "k"):
        args += [gain, gmat]
        in_specs += [pl.BlockSpec((1, n), const), pl.BlockSpec((MXU_DIM, MXU_DIM), const)]
        if rope:
            args += list(rope_tabs)
            in_specs += [pl.BlockSpec((tr, LANES), lambda j: (done(j)[1], 0))] * 3
    if mode == "g":
        out_specs = [pl.BlockSpec((1, tr, n), lambda j: (*done(j), 0))]
        out_shape = [jax.ShapeDtypeStruct((nb, l, n), BF16)]
    elif mode == "v":
        out_specs = [pl.BlockSpec((1, nheads, V_ROWS, tr), lambda j: (done(j)[0], 0, 0, done(j)[1]))]
        out_shape = [jax.ShapeDtypeStruct((nb, nheads, V_ROWS, l), BF16)]
    else:
        hspec = pl.BlockSpec((1, nheads, tr, LANES), lambda j: (done(j)[0], 0, done(j)[1], 0))
        hshape = jax.ShapeDtypeStruct((nb, nheads, l, LANES), BF16)
        out_specs, out_shape = [hspec], [hshape]
    kern = functools.partial(_att_proj_kernel, mode=mode, rope=rope, nheads=nheads)
    return pl.pallas_call(
        kern,
        grid=(ntiles + lag,),
        in_specs=in_specs,
        out_specs=out_specs,
        out_shape=out_shape,
        scratch_shapes=scratch,
        compiler_params=_params(("arbitrary",)),
        name="att_proj_" + mode + ("_rope" if rope else ""),
    )(*args)


def _attn_kernel(q_ref, kl_ref, kc_ref, vtl_ref, vtc_ref, lv_ref, sg_ref, o_ref, st_ref, m_ref, p_ref, *,
                 lam_init, tq):
    @pl.when(pl.program_id(0) == 0)
    def _():
        st_ref[...] = jnp.zeros_like(st_ref)
        m_ref[...] = jnp.zeros_like(m_ref)
        p_ref[...] = jnp.ones_like(p_ref)

    lv = lv_ref[...]
    lam = (jnp.exp(jnp.sum(lv[0:1] * lv[1:2], axis=-1, keepdims=True))
           - jnp.exp(jnp.sum(lv[2:3] * lv[3:4], axis=-1, keepdims=True)) + lam_init)
    s = kl_ref.shape[2]
    ot = _dot(vtl_ref[0, 0], p_ref[0:s, :]) + _dot(vtc_ref[0, 0], p_ref[s:, :])
    on = ot[0:LANES] * (1.0 / ot[LANES:LANES + 1])
    od = on[:, 0:tq] - lam * on[:, tq:2 * tq]
    ms = jnp.mean(od * od, axis=0, keepdims=True)
    o_ref[0, 0] = (od * lax.rsqrt(ms + EPS) * sg_ref[...]).T.astype(BF16)

    p_ref[...] = jnp.exp2((st_ref[...] - m_ref[...]).astype(BF16))

    qh = q_ref[0, 0]
    first = lax.broadcasted_iota(jnp.int32, qh.shape, 1) < HEAD_DIM
    zero = jnp.zeros_like(qh)
    q = jnp.concatenate([jnp.where(first, qh, zero), jnp.where(first, zero, qh)], axis=0)
    st_l = lax.dot_general(kl_ref[0, 0], q, _NT, preferred_element_type=F32)
    st_c = lax.dot_general(kc_ref[0, 0], q, _NT, preferred_element_type=F32)
    st_ref[0:s, :] = st_l
    st_ref[s:, :] = st_c
    m_ref[...] = jnp.maximum(jnp.max(st_l, axis=0, keepdims=True), jnp.max(st_c, axis=0, keepdims=True))


def _attention(qh, k_lat, k_ctx, vt_lat, vt_ctx, lam_vecs, subln_t, *, lam_init):
    nb, nh, s, _ = qh.shape
    clen = k_ctx.shape[2]
    ltot = s + clen
    tq = subln_t.shape[1]
    nq = s // tq
    ntiles = nb * nh * nq

    def tile(t):
        return t // (nh * nq), (t // nq) % nh, t % nq

    def q_map(i):
        b, h, qi = tile(jnp.minimum(i, ntiles - 1))
        return b, h, qi, 0

    def k_map(i):
        b, h, _ = tile(jnp.minimum(i, ntiles - 1))
        return b, h, 0, 0

    def v_map(i):
        b, h, _ = tile(jnp.maximum(i - 2, 0))
        return b, h, 0, 0

    def o_map(i):
        b, h, qi = tile(jnp.maximum(i - 2, 0))
        return b, h, qi, 0

    qspec = pl.BlockSpec((1, 1, tq, LANES), q_map)
    return pl.pallas_call(
        functools.partial(_attn_kernel, lam_init=lam_init, tq=tq),
        grid=(ntiles + 2,),
        in_specs=[qspec,
                  pl.BlockSpec((1, 1, s, LANES), k_map),
                  pl.BlockSpec((1, 1, clen, LANES), k_map),
                  pl.BlockSpec((1, 1, V_ROWS, s), v_map),
                  pl.BlockSpec((1, 1, V_ROWS, clen), v_map),
                  pl.BlockSpec((4, HEAD_DIM), lambda i: (0, 0)),
                  pl.BlockSpec((LANES, tq), lambda i: (0, 0))],
        out_specs=pl.BlockSpec((1, 1, tq, LANES), o_map),
        out_shape=jax.ShapeDtypeStruct((nb, nh, s, LANES), BF16),
        scratch_shapes=[pltpu.VMEM((ltot, 2 * tq), F32), pltpu.VMEM((1, 2 * tq), F32),
                        pltpu.VMEM((ltot, 2 * tq), BF16)],
        compiler_params=_params(("arbitrary",)),
        name="diff_attention",
    )(qh, k_lat, k_ctx, vt_lat, vt_ctx, lam_vecs, subln_t)


def _att_out_kernel(o_ref, gs_ref, w_ref, x_ref, gate_ref, out_ref, *, nheads):
    o = jnp.concatenate([o_ref[0, h] for h in range(nheads)], axis=1)
    z = (o.astype(F32) * gs_ref[0].astype(F32)).astype(BF16)
    out_ref[0] = x_ref[0] + gate_ref[0] * _dot(z, w_ref[...])


def _att_out(o_heads, gs, w_out, x1, gate):
    nb, nh, s, _ = o_heads.shape
    d = w_out.shape[1]
    wd = w_out.shape[0]
    tr = min(512, s)
    return pl.pallas_call(
        functools.partial(_att_out_kernel, nheads=nh),
        grid=(nb, s // tr),
        in_specs=[
            pl.BlockSpec((1, nh, tr, LANES), lambda b, i: (b, 0, i, 0)),
            pl.BlockSpec((1, tr, wd), lambda b, i: (b, i, 0)),
            _resident((wd, d)),
            pl.BlockSpec((1, tr, d), lambda b, i: (b, i, 0)),
            pl.BlockSpec((1, 1, d), lambda b, i: (b, 0, 0)),
        ],
        out_specs=pl.BlockSpec((1, tr, d), lambda b, i: (b, i, 0)),
        out_shape=jax.ShapeDtypeStruct((nb, s, d), F32),
        compiler_params=_params(("parallel", "parallel")),
        name="att_out",
    )(o_heads, gs, w_out, x1, gate)


def _gate_weights(gw, gb):
    w = 0.5 * jnp.concatenate([gw[0], gw[1]], axis=-1)
    b = 0.5 * jnp.concatenate([gb[0], gb[1]], axis=-1)
    b_hi = b.astype(BF16)
    b_lo = (b - b_hi.astype(F32)).astype(BF16)
    pad = jnp.zeros((w.shape[0], LANES - 2, 2 * LANES), BF16)
    return jnp.concatenate([w.astype(BF16), b_hi[:, None, :], b_lo[:, None, :], pad], axis=1)


def _rope_tables(s):
    t = jnp.arange(s)
    pos = jnp.stack([(t // GRID_W).astype(F32), (t % GRID_W).astype(F32)], axis=1)
    inv = ROPE_BASE ** (-jnp.arange(ROPE_FREQS, dtype=F32) / ROPE_FREQS)
    ang = pos[:, :, None] * inv
    lane = jnp.arange(LANES)
    dd = lane % HEAD_DIM
    axis, half, f = dd // (2 * ROPE_FREQS), (dd % (2 * ROPE_FREQS)) // ROPE_FREQS, dd % ROPE_FREQS
    cos = jnp.cos(ang)[:, axis, f]
    sin = jnp.sin(ang)[:, axis, f]
    return cos, jnp.where(half == 0, -sin, 0.0), jnp.where(half == 1, sin, 0.0)


def kernel(x, c, ctx, c_ctx, mod_w, mod_b, norm_g, lru_w_in, lru_conv_w, lru_conv_b, lru_gate_w, lru_gate_b,
           lru_lambda, lru_w_out, att_w_in, att_q_norm, att_k_norm, att_lambda, att_subln, att_w_out):
    nb, s, d = x.shape
    clen = ctx.shape[1]
    aw = att_w_out.shape[1]
    assert lru_gate_w.shape[-1] == LANES and d % MXU_DIM == 0 and nb % 8 == 0

    rows = ((nb + 1 + 7) // 8) * 8
    cs = jnp.zeros((rows, d), F32).at[:nb].set(c).at[nb].set(c_ctx)
    mod = _modulation(cs, mod_w, mod_b)

    def split(i):
        lat = [mod[i, :nb, k * d:(k + 1) * d].reshape(nb, 1, d) for k in range(3)]
        cx = [jnp.broadcast_to(mod[i, nb:nb + 1, k * d:(k + 1) * d], (nb, d)).reshape(nb, 1, d) for k in range(3)]
        return lat, cx

    (shift0, scale0, gate0), (cshift0, cscale0, cgate0) = split(0)
    (shift1, scale1, gate1), (cshift1, cscale1, _) = split(1)

    w_in0 = lru_w_in[0].astype(BF16)
    ltot = s + clen
    both = lambda a, b: jnp.stack([a.reshape(nb, d), b.reshape(nb, d)])
    u_tm, gs_tm = _lru_in(ctx, x, both(cshift0, shift0), both(cscale0, scale0), norm_g[0:1], w_in0)
    hs = []
    for dr in range(2):
        hs.append(_lru_scan(u_tm, lru_conv_w[0], lru_conv_b[0:1], _gate_weights(lru_gate_w[0, dr], lru_gate_b[0, dr]),
                            lru_lambda[0, dr:dr + 1], nb=nb, c=clen, reverse=bool(dr)))
    w_out0 = lru_w_out[0].astype(BF16)
    g1 = norm_g[1:2]
    _, h1_ctx = _lru_out(hs[0], hs[1], gs_tm, w_out0, ctx, cgate0, g1, cshift1, cscale1, row_off=0)
    x1, h1_lat = _lru_out(hs[0], hs[1], gs_tm, w_out0, x, gate0, g1, shift1, scale1, row_off=clen)

    lam_init = 0.8 - 0.6 * math.exp(-0.3 * 1)
    wq, wk, wv, wg = (att_w_in[0][:, k * aw:(k + 1) * aw].astype(BF16) for k in range(4))
    gmat = jnp.kron(jnp.eye(MXU_DIM // HEAD_DIM, dtype=F32),
                    jnp.full((HEAD_DIM, HEAD_DIM), 1.0 / HEAD_DIM, F32)).astype(BF16)
    reps = aw // HEAD_DIM
    q_gain = jnp.tile(att_q_norm[0] * (HEAD_DIM ** -0.5 * math.log2(math.e)), reps)[None, :]
    k_gain = jnp.tile(att_k_norm[0], reps)[None, :]
    tabs = _rope_tables(s)
    qh, = _att_proj(h1_lat, wq, mode="q", gain=q_gain, gmat=gmat, rope_tabs=tabs)
    k_lat, = _att_proj(h1_lat, wk, mode="k", gain=k_gain, gmat=gmat, rope_tabs=tabs)
    k_ctx, = _att_proj(h1_ctx, wk, mode="k", gain=k_gain, gmat=gmat)
    vt_lat, = _att_proj(h1_lat, wv.T, mode="v")
    vt_ctx, = _att_proj(h1_ctx, wv.T, mode="v")
    gsa, = _att_proj(h1_lat, wg, mode="g")
    subln_t = jnp.broadcast_to((att_subln[0] * (1.0 - lam_init))[:, None], (LANES, min(ATT_TQ, s)))
    o_heads = _attention(qh, k_lat, k_ctx, vt_lat, vt_ctx, att_lambda[0], subln_t, lam_init=lam_init)
    return _att_out(o_heads, gsa, att_w_out[0].astype(BF16), x1, gate1)
```

```python
import functools
import math

import jax
import jax.numpy as jnp
from jax import lax
from jax.experimental import pallas as pl
from jax.experimental.pallas import tpu as pltpu

F32 = jnp.float32
BF16 = jnp.bfloat16

EPS = 1e-6
LRU_C = 8.0
GRID_W = 64
HEAD_DIM = 64
ROPE_FREQS = HEAD_DIM // 4
ROPE_BASE = 10000.0
LANES = 128
MXU_DIM = 256
VMEM_LIMIT = 56 * 1024 * 1024
V_ROWS = LANES + 16
ATT_TQ = 1024
LRU_TT = 32
LRU_SUB = 1
_NT = (((1,), (1,)), ((), ()))


def _params(sem, vmem=VMEM_LIMIT):
    return pltpu.CompilerParams(dimension_semantics=sem, vmem_limit_bytes=vmem)


def _sigmoid(x):
    return 1.0 / (1.0 + jnp.exp(-x))


def _silu(x):
    return x * _sigmoid(x)


def _dot(a, b):
    return jnp.dot(a, b, preferred_element_type=F32)


def _resident(shape):
    nd = len(shape)
    return pl.BlockSpec(shape, lambda *_: (0,) * nd, pipeline_mode=pl.Buffered(1))


def _mod_kernel(cs_ref, w_ref, b_ref, o_ref):
    a = _silu(cs_ref[...])
    w = w_ref[0]
    a_hi = a.astype(BF16)
    a_lo = (a - a_hi.astype(F32)).astype(BF16)
    w_hi = w.astype(BF16)
    w_lo = (w - w_hi.astype(F32)).astype(BF16)
    acc = _dot(a_hi, w_hi) + _dot(a_lo, w_hi) + _dot(a_hi, w_lo)
    o_ref[0] = acc + b_ref[0]


def _modulation(cs, mod_w, mod_b):
    depth, d, n3 = mod_w.shape
    rows = cs.shape[0]
    tn = 768 if n3 % 768 == 0 else n3
    return pl.pallas_call(
        _mod_kernel,
        grid=(depth, n3 // tn),
        in_specs=[
            pl.BlockSpec((rows, d), lambda i, n: (0, 0)),
            pl.BlockSpec((1, d, tn), lambda i, n: (i, 0, n)),
            pl.BlockSpec((1, 1, tn), lambda i, n: (i, 0, n)),
        ],
        out_specs=pl.BlockSpec((1, rows, tn), lambda i, n: (i, 0, n)),
        out_shape=jax.ShapeDtypeStruct((depth, rows, n3), F32),
        compiler_params=_params(("parallel", "parallel")),
        name="modulation",
    )(cs, mod_w, mod_b.reshape(depth, 1, n3))


def _norm_mod(xb, g, scale, shift):
    ms = jnp.mean(xb * xb, axis=-1, keepdims=True)
    return (xb * lax.rsqrt(ms + EPS) * g) * (1.0 + scale) + shift


def _slab_pitch(nb):
    return nb + 8


def _lru_in_kernel(ctx_ref, x_ref, sh_ref, sc_ref, g_ref, w_ref, u_ref, gs_ref, *slabs, nc, nb, tt, d, w, nch):
    is_ctx = pl.program_id(0) < nc
    m = tt * nb
    pitch = _slab_pitch(nb)
    g = g_ref[...]
    for k, slab_ref in enumerate(slabs):
        for b in range(nb):
            xb = jnp.where(is_ctx, ctx_ref[b, k * tt:(k + 1) * tt, :], x_ref[b, k * tt:(k + 1) * tt, :])
            h = _norm_mod(xb, g, sc_ref[0, b:b + 1, :], sh_ref[0, b:b + 1, :])
            for s in range(d // LANES):
                slab_ref[s, pl.ds(b, tt, stride=pitch), :] = h[:, s * LANES:(s + 1) * LANES]
        hfull = jnp.concatenate(
            [jnp.concatenate([slab_ref[s, t * pitch:t * pitch + nb, :] for t in range(tt)], axis=0)
             for s in range(d // LANES)], axis=1).astype(BF16)
        rows = slice(k * m, (k + 1) * m)
        for n in range(0, 2 * w, nch):
            res = _dot(hfull, w_ref[:, n:n + nch])
            if n < w:
                u_ref[rows, n:n + nch] = res.astype(BF16)
            else:
                gs_ref[rows, n - w:n - w + nch] = _silu(res).astype(BF16)


def _lru_in(ctx, x, shift2, scale2, g, w_in):
    nb, c, d = ctx.shape
    s = x.shape[1]
    w = w_in.shape[1] // 2
    tt, sub = LRU_TT, LRU_SUB
    step = sub * tt
    assert c % step == 0 and s % step == 0
    nc = c // step
    m = tt * nb
    nch = min(512, w)
    kern = functools.partial(_lru_in_kernel, nc=nc, nb=nb, tt=tt, d=d, w=w, nch=nch)
    mod = pl.BlockSpec((1, nb, d), lambda j: (jnp.where(j < nc, 0, 1), 0, 0))
    out = jax.ShapeDtypeStruct(((c + s) * nb, w), BF16)
    tiles = pl.BlockSpec((sub * m, w), lambda j: (j, 0))
    return pl.pallas_call(
        kern,
        grid=((c + s) // step,),
        in_specs=[
            pl.BlockSpec((nb, step, d), lambda j: (0, jnp.minimum(j, nc - 1), 0)),
            pl.BlockSpec((nb, step, d), lambda j: (0, jnp.maximum(j - nc, 0), 0)),
            mod, mod, pl.BlockSpec((1, d), lambda j: (0, 0)),
            _resident((d, 2 * w)),
        ],
        out_specs=[tiles, tiles],
        out_shape=[out, out],
        scratch_shapes=[pltpu.VMEM((d // LANES, tt * _slab_pitch(nb), LANES), F32)] * sub,
        compiler_params=_params(("parallel",)),
        name="lru_in",
    )(ctx, x, shift2, scale2, g, w_in)


def _scan_chunk(j, nc, ntot, reverse):
    if not reverse:
        return j
    return jnp.where(j < nc, nc - 1 - j, ntot - 1 - j + nc)


def _lru_scan_kernel(prev_ref, cur_ref, next_ref, cw_ref, cb_ref, gw_ref, lam_ref, hs_ref,
                     a_ref, b_ref, h_ref, *, nc, ntot, nb, tc, wb, reverse):
    j = pl.program_id(1)
    c = _scan_chunk(j, nc, ntot, reverse)
    seq_start = (c == 0) | (c == nc)
    seq_end = (c == nc - 1) | (c == ntot - 1)
    m = tc * nb

    @pl.when(j == 0)
    def _():
        h_ref[...] = jnp.zeros_like(h_ref)

    prev = prev_ref[...]
    nxt = next_ref[0:nb, :]
    ext = jnp.concatenate([jnp.where(seq_start, jnp.zeros_like(prev), prev), cur_ref[...],
                           jnp.where(seq_end, jnp.zeros_like(nxt), nxt)], axis=0)
    cw = cw_ref[...].astype(BF16)
    xc = cb_ref[...].astype(BF16) + cw[0:1] * ext[0:m]
    for k in range(1, 4):
        xc = xc + cw[k:k + 1] * ext[k * nb:k * nb + m]

    lam = lam_ref[...]
    half_c = (0.5 * LRU_C) * (jnp.maximum(-lam, 0.0) + jnp.log1p(jnp.exp(-jnp.abs(lam))))
    nslab = wb // LANES
    ones = jnp.ones((m, LANES), BF16)
    for s in range(nslab):
        xsb = xc[:, s * LANES:(s + 1) * LANES]
        xs = xsb.astype(F32)
        th = jnp.tanh(_dot(jnp.concatenate([xsb, ones], axis=1), gw_ref[s]))
        hc = half_c[:, s * LANES:(s + 1) * LANES]
        nla = th[:, :LANES] * hc + hc
        a = jnp.exp2(nla * (-math.log2(math.e)))
        a_ref[s] = a
        y = jnp.tanh(nla) * (a * a + 1.0)
        sq = jnp.where(y > 0.0, y * lax.rsqrt(y), 0.0)
        b_ref[s] = sq * ((0.5 * th[:, LANES:] + 0.5) * xs)

    def step(i, hs):
        t = (tc - 1 - i) if reverse else i
        rows = pl.ds(pl.multiple_of(t * nb, nb), nb)
        out = []
        for s in range(nslab):
            h = a_ref[s, rows, :] * hs[s] + b_ref[s, rows, :]
            hs_ref[rows, s * LANES:(s + 1) * LANES] = h.astype(BF16)
            out.append(h)
        return tuple(out)

    hs = lax.fori_loop(0, tc, step, tuple(h_ref[s] for s in range(nslab)), unroll=4)
    for s in range(nslab):
        h_ref[s] = hs[s]


def _lru_scan(u_tm, conv_w, conv_b, gw, lam, *, nb, c, reverse):
    rows, w = u_tm.shape
    ltot = rows // nb
    tc = math.gcd(c, 64)
    wb = min(512, w)
    nc, ntot = c // tc, ltot // tc
    m = tc * nb
    hb = tc // 2
    last_halo = ltot // 2 - 1
    chunk = functools.partial(_scan_chunk, nc=nc, ntot=ntot, reverse=reverse)
    kern = functools.partial(_lru_scan_kernel, nc=nc, ntot=ntot, nb=nb, tc=tc, wb=wb, reverse=reverse)
    nslab = wb // LANES
    return pl.pallas_call(
        kern,
        grid=(w // wb, ntot),
        in_specs=[
            pl.BlockSpec((2 * nb, wb), lambda i, j: (jnp.maximum(chunk(j) * hb - 1, 0), i)),
            pl.BlockSpec((m, wb), lambda i, j: (chunk(j), i)),
            pl.BlockSpec((2 * nb, wb), lambda i, j: (jnp.minimum((chunk(j) + 1) * hb, last_halo), i)),
            pl.BlockSpec((4, wb), lambda i, j: (0, i)),
            pl.BlockSpec((1, wb), lambda i, j: (0, i)),
            pl.BlockSpec((nslab, 2 * LANES, 2 * LANES), lambda i, j: (i, 0, 0)),
            pl.BlockSpec((1, wb), lambda i, j: (0, i)),
        ],
        out_specs=pl.BlockSpec((m, wb), lambda i, j: (chunk(j), i)),
        out_shape=jax.ShapeDtypeStruct((rows, w), BF16),
        scratch_shapes=[
            pltpu.VMEM((nslab, m, LANES), F32),
            pltpu.VMEM((nslab, m, LANES), F32),
            pltpu.VMEM((nslab, nb, LANES), F32),
        ],
        compiler_params=_params(("parallel", "arbitrary")),
        name="lru_scan_bwd" if reverse else "lru_scan_fwd",
    )(u_tm, u_tm, u_tm, conv_w, conv_b, gw, lam)


def _lru_out_kernel(hf_ref, hb_ref, gs_ref, w_ref, x_ref, gate_ref, g_ref, sh_ref, sc_ref, x1_ref, h1_ref,
                    *slabs, nb, tt, d, nch):
    m = tt * nb
    per = nch // LANES
    pitch = _slab_pitch(nb)
    g = g_ref[...]
    for k, slab_ref in enumerate(slabs):
        rows = slice(k * m, (k + 1) * m)
        z = ((hf_ref[rows, :].astype(F32) + hb_ref[rows, :].astype(F32)) * gs_ref[rows, :].astype(F32)).astype(BF16)
        for c in range(d // nch):
            y = _dot(z, w_ref[:, c * nch:(c + 1) * nch])
            for j in range(per):
                for t in range(tt):
                    slab_ref[c * per + j, t * pitch:t * pitch + nb, :] = y[t * nb:(t + 1) * nb, j * LANES:(j + 1) * LANES]
        t0 = k * tt
        for b in range(nb):
            yb = jnp.concatenate([slab_ref[s, pl.ds(b, tt, stride=pitch), :] for s in range(d // LANES)], axis=1)
            x1 = x_ref[b, t0:t0 + tt, :] + gate_ref[b] * yb
            x1_ref[b, t0:t0 + tt, :] = x1
            h1_ref[b, t0:t0 + tt, :] = _norm_mod(x1, g, sc_ref[b], sh_ref[b]).astype(BF16)


def _lru_out(hs_f, hs_b, gs, w_out, xres, gate, g1, shift1, scale1, *, row_off):
    nb, l, d = xres.shape
    w = w_out.shape[0]
    tt, sub = LRU_TT, LRU_SUB
    step = sub * tt
    assert l % step == 0 and row_off % step == 0
    m = tt * nb
    off = row_off // step
    kern = functools.partial(_lru_out_kernel, nb=nb, tt=tt, d=d, nch=min(512, d))
    bmod = pl.BlockSpec((nb, 1, d), lambda j: (0, 0, 0))
    act = pl.BlockSpec((sub * m, w), lambda j: (j + off, 0))
    res = pl.BlockSpec((nb, step, d), lambda j: (0, j, 0))
    return pl.pallas_call(
        kern,
        grid=(l // step,),
        in_specs=[act, act, act, _resident((w, d)), res, bmod, pl.BlockSpec((1, d), lambda j: (0, 0)), bmod, bmod],
        out_specs=[res, res],
        out_shape=[jax.ShapeDtypeStruct((nb, l, d), F32), jax.ShapeDtypeStruct((nb, l, d), BF16)],
        scratch_shapes=[pltpu.VMEM((d // LANES, tt * _slab_pitch(nb), LANES), F32)] * sub,
        compiler_params=_params(("parallel",)),
        name="lru_out",
    )(hs_f, hs_b, gs, w_out, xres, gate, g1, shift1, scale1)


def _att_proj_kernel(*refs, mode, rope, nheads):
    refs = list(refs)
    h_ref, w_ref = refs[0], refs[1]
    pos = 2
    if mode in ("q", "k"):
        gain_ref, gmat_ref = refs[pos], refs[pos + 1]
        pos += 2
        if rope:
            cos_ref, sa_ref, sb_ref = refs[pos:pos + 3]
            pos += 3
    outs = refs[pos:]
    if mode == "v":
        res_t = lax.dot_general(w_ref[...], h_ref[0], _NT, preferred_element_type=F32)
        ones = jnp.ones((V_ROWS - LANES, res_t.shape[1]), BF16)
        for h in range(nheads):
            outs[0][0, h, 0:LANES, :] = res_t[h * LANES:(h + 1) * LANES, :].astype(BF16)
            outs[0][0, h, LANES:V_ROWS, :] = ones
        return
    res_ref = outs.pop()

    @pl.when(pl.program_id(0) == 0)
    def _():
        res_ref[...] = jnp.zeros_like(res_ref)

    for p in range(res_ref.shape[1] // MXU_DIM):
        r = res_ref[:, p * MXU_DIM:(p + 1) * MXU_DIM]
        if mode == "g":
            outs[0][0, :, p * MXU_DIM:(p + 1) * MXU_DIM] = _silu(r).astype(BF16)
            continue
        ms = _dot((r * r).astype(BF16), gmat_ref[...])
        rn = r * lax.rsqrt(ms + EPS) * gain_ref[:, p * MXU_DIM:(p + 1) * MXU_DIM]
        for hh in range(MXU_DIM // LANES):
            xh = rn[:, hh * LANES:(hh + 1) * LANES]
            if rope:
                xh = (xh * cos_ref[...] + pltpu.roll(xh, LANES - ROPE_FREQS, 1) * sa_ref[...]
                      + pltpu.roll(xh, ROPE_FREQS, 1) * sb_ref[...])
            outs[0][0, p * (MXU_DIM // LANES) + hh] = xh.astype(BF16)
    res_ref[...] = _dot(h_ref[0], w_ref[...])


def _att_proj(h1, w, *, mode, gain=None, gmat=None, rope_tabs=None):
    nb, l, d = h1.shape
    n = w.shape[0] if mode == "v" else w.shape[1]
    tr = min(512, l)
    nheads = n // LANES
    rope = rope_tabs is not None
    nt = l // tr
    ntiles = nb * nt
    lag = 0 if mode == "v" else 1

    def cur(j):
        t = jnp.minimum(j, ntiles - 1)
        return t // nt, t % nt

    def done(j):
        t = jnp.maximum(j - lag, 0)
        return t // nt, t % nt

    const = lambda j: (0, 0)
    args = [h1, w]
    in_specs = [pl.BlockSpec((1, tr, d), lambda j: (*cur(j), 0)), _resident(w.shape)]
    scratch = [pltpu.VMEM((tr, n), F32)] if lag else []
    if mode in ("q", "k"):
        args += [gain, gmat]
        in_specs += [pl.BlockSpec((1, n), const), pl.BlockSpec((MXU_DIM, MXU_DIM), const)]
        if rope:
            args += list(rope_tabs)
            in_specs += [pl.BlockSpec((tr, LANES), lambda j: (done(j)[1], 0))] * 3
    if mode == "g":
        out_specs = [pl.BlockSpec((1, tr, n), lambda j: (*done(j), 0))]
        out_shape = [jax.ShapeDtypeStruct((nb, l, n), BF16)]
    elif mode == "v":
        out_specs = [pl.BlockSpec((1, nheads, V_ROWS, tr), lambda j: (done(j)[0], 0, 0, done(j)[1]))]
        out_shape = [jax.ShapeDtypeStruct((nb, nheads, V_ROWS, l), BF16)]
    else:
        hspec = pl.BlockSpec((1, nheads, tr, LANES), lambda j: (done(j)[0], 0, done(j)[1], 0))
        hshape = jax.ShapeDtypeStruct((nb, nheads, l, LANES), BF16)
        out_specs, out_shape = [hspec], [hshape]
    kern = functools.partial(_att_proj_kernel, mode=mode, rope=rope, nheads=nheads)
    return pl.pallas_call(
        kern,
        grid=(ntiles + lag,),
        in_specs=in_specs,
        out_specs=out_specs,
        out_shape=out_shape,
        scratch_shapes=scratch,
        compiler_params=_params(("arbitrary",)),
        name="att_proj_" + mode + ("_rope" if rope else ""),
    )(*args)


def _attn_kernel(q_ref, kl_ref, kc_ref, vtl_ref, vtc_ref, lv_ref, sg_ref, o_ref, st_ref, m_ref, p_ref, *,
                 lam_init, tq):
    @pl.when(pl.program_id(0) == 0)
    def _():
        st_ref[...] = jnp.zeros_like(st_ref)
        m_ref[...] = jnp.zeros_like(m_ref)
        p_ref[...] = jnp.ones_like(p_ref)

    lv = lv_ref[...]
    lam = (jnp.exp(jnp.sum(lv[0:1] * lv[1:2], axis=-1, keepdims=True))
           - jnp.exp(jnp.sum(lv[2:3] * lv[3:4], axis=-1, keepdims=True)) + lam_init)
    s = kl_ref.shape[2]
    ot = _dot(vtl_ref[0, 0], p_ref[0:s, :]) + _dot(vtc_ref[0, 0], p_ref[s:, :])
    on = ot[0:LANES] * (1.0 / ot[LANES:LANES + 1])
    od = on[:, 0:tq] - lam * on[:, tq:2 * tq]
    ms = jnp.mean(od * od, axis=0, keepdims=True)
    o_ref[0, 0] = (od * lax.rsqrt(ms + EPS) * sg_ref[...]).T.astype(BF16)

    p_ref[...] = jnp.exp2((st_ref[...] - m_ref[...]).astype(BF16))

    qh = q_ref[0, 0]
    first = lax.broadcasted_iota(jnp.int32, qh.shape, 1) < HEAD_DIM
    zero = jnp.zeros_like(qh)
    q = jnp.concatenate([jnp.where(first, qh, zero), jnp.where(first, zero, qh)], axis=0)
    st_l = lax.dot_general(kl_ref[0, 0], q, _NT, preferred_element_type=F32)
    st_c = lax.dot_general(kc_ref[0, 0], q, _NT, preferred_element_type=F32)
    st_ref[0:s, :] = st_l
    st_ref[s:, :] = st_c
    m_ref[...] = jnp.maximum(jnp.max(st_l, axis=0, keepdims=True), jnp.max(st_c, axis=0, keepdims=True))


def _attention(qh, k_lat, k_ctx, vt_lat, vt_ctx, lam_vecs, subln_t, *, lam_init):
    nb, nh, s, _ = qh.shape
    clen = k_ctx.shape[2]
    ltot = s + clen
    tq = subln_t.shape[1]
    nq = s // tq
    ntiles = nb * nh * nq

    def tile(t):
        return t // (nh * nq), (t // nq) % nh, t % nq

    def q_map(i):
        b, h, qi = tile(jnp.minimum(i, ntiles - 1))
        return b, h, qi, 0

    def k_map(i):
        b, h, _ = tile(jnp.minimum(i, ntiles - 1))
        return b, h, 0, 0

    def v_map(i):
        b, h, _ = tile(jnp.maximum(i - 2, 0))
        return b, h, 0, 0

    def o_map(i):
        b, h, qi = tile(jnp.maximum(i - 2, 0))
        return b, h, qi, 0

    qspec = pl.BlockSpec((1, 1, tq, LANES), q_map)
    return pl.pallas_call(
        functools.partial(_attn_kernel, lam_init=lam_init, tq=tq),
        grid=(ntiles + 2,),
        in_specs=[qspec,
                  pl.BlockSpec((1, 1, s, LANES), k_map),
                  pl.BlockSpec((1, 1, clen, LANES), k_map),
                  pl.BlockSpec((1, 1, V_ROWS, s), v_map),
                  pl.BlockSpec((1, 1, V_ROWS, clen), v_map),
                  pl.BlockSpec((4, HEAD_DIM), lambda i: (0, 0)),
                  pl.BlockSpec((LANES, tq), lambda i: (0, 0))],
        out_specs=pl.BlockSpec((1, 1, tq, LANES), o_map),
        out_shape=jax.ShapeDtypeStruct((nb, nh, s, LANES), BF16),
        scratch_shapes=[pltpu.VMEM((ltot, 2 * tq), F32), pltpu.VMEM((1, 2 * tq), F32),
                        pltpu.VMEM((ltot, 2 * tq), BF16)],
        compiler_params=_params(("arbitrary",)),
        name="diff_attention",
    )(qh, k_lat, k_ctx, vt_lat, vt_ctx, lam_vecs, subln_t)


def _att_out_kernel(o_ref, gs_ref, w_ref, x_ref, gate_ref, out_ref, *, nheads):
    o = jnp.concatenate([o_ref[0, h] for h in range(nheads)], axis=1)
    z = (o.astype(F32) * gs_ref[0].astype(F32)).astype(BF16)
    out_ref[0] = x_ref[0] + gate_ref[0] * _dot(z, w_ref[...])


def _att_out(o_heads, gs, w_out, x1, gate):
    nb, nh, s, _ = o_heads.shape
    d = w_out.shape[1]
    wd = w_out.shape[0]
    tr = min(512, s)
    return pl.pallas_call(
        functools.partial(_att_out_kernel, nheads=nh),
        grid=(nb, s // tr),
        in_specs=[
            pl.BlockSpec((1, nh, tr, LANES), lambda b, i: (b, 0, i, 0)),
            pl.BlockSpec((1, tr, wd), lambda b, i: (b, i, 0)),
            _resident((wd, d)),
            pl.BlockSpec((1, tr, d), lambda b, i: (b, i, 0)),
            pl.BlockSpec((1, 1, d), lambda b, i: (b, 0, 0)),
        ],
        out_specs=pl.BlockSpec((1, tr, d), lambda b, i: (b, i, 0)),
        out_shape=jax.ShapeDtypeStruct((nb, s, d), F32),
        compiler_params=_params(("parallel", "parallel")),
        name="att_out",
    )(o_heads, gs, w_out, x1, gate)


def _gate_weights(gw, gb):
    w = 0.5 * jnp.concatenate([gw[0], gw[1]], axis=-1)
    b = 0.5 * jnp.concatenate([gb[0], gb[1]], axis=-1)
    b_hi = b.astype(BF16)
    b_lo = (b - b_hi.astype(F32)).astype(BF16)
    pad = jnp.zeros((w.shape[0], LANES - 2, 2 * LANES), BF16)
    return jnp.concatenate([w.astype(BF16), b_hi[:, None, :], b_lo[:, None, :], pad], axis=1)


def _rope_tables(s):
    t = jnp.arange(s)
    pos = jnp.stack([(t // GRID_W).astype(F32), (t % GRID_W).astype(F32)], axis=1)
    inv = ROPE_BASE ** (-jnp.arange(ROPE_FREQS, dtype=F32) / ROPE_FREQS)
    ang = pos[:, :, None] * inv
    lane = jnp.arange(LANES)
    dd = lane % HEAD_DIM
    axis, half, f = dd // (2 * ROPE_FREQS), (dd % (2 * ROPE_FREQS)) // ROPE_FREQS, dd % ROPE_FREQS
    cos = jnp.cos(ang)[:, axis, f]
    sin = jnp.sin(ang)[:, axis, f]
    return cos, jnp.where(half == 0, -sin, 0.0), jnp.where(half == 1, sin, 0.0)


def kernel(x, c, ctx, c_ctx, mod_w, mod_b, norm_g, lru_w_in, lru_conv_w, lru_conv_b, lru_gate_w, lru_gate_b,
           lru_lambda, lru_w_out, att_w_in, att_q_norm, att_k_norm, att_lambda, att_subln, att_w_out):
    nb, s, d = x.shape
    clen = ctx.shape[1]
    aw = att_w_out.shape[1]
    assert lru_gate_w.shape[-1] == LANES and d % MXU_DIM == 0 and nb % 8 == 0

    rows = ((nb + 1 + 7) // 8) * 8
    cs = jnp.zeros((rows, d), F32).at[:nb].set(c).at[nb].set(c_ctx)
    mod = _modulation(cs, mod_w, mod_b)

    def split(i):
        lat = [mod[i, :nb, k * d:(k + 1) * d].reshape(nb, 1, d) for k in range(3)]
        cx = [jnp.broadcast_to(mod[i, nb:nb + 1, k * d:(k + 1) * d], (nb, d)).reshape(nb, 1, d) for k in range(3)]
        return lat, cx

    (shift0, scale0, gate0), (cshift0, cscale0, cgate0) = split(0)
    (shift1, scale1, gate1), (cshift1, cscale1, _) = split(1)

    w_in0 = lru_w_in[0].astype(BF16)
    ltot = s + clen
    both = lambda a, b: jnp.stack([a.reshape(nb, d), b.reshape(nb, d)])
    u_tm, gs_tm = _lru_in(ctx, x, both(cshift0, shift0), both(cscale0, scale0), norm_g[0:1], w_in0)
    hs = []
    for dr in range(2):
        hs.append(_lru_scan(u_tm, lru_conv_w[0], lru_conv_b[0:1], _gate_weights(lru_gate_w[0, dr], lru_gate_b[0, dr]),
                            lru_lambda[0, dr:dr + 1], nb=nb, c=clen, reverse=bool(dr)))
    w_out0 = lru_w_out[0].astype(BF16)
    g1 = norm_g[1:2]
    _, h1_ctx = _lru_out(hs[0], hs[1], gs_tm, w_out0, ctx, cgate0, g1, cshift1, cscale1, row_off=0)
    x1, h1_lat = _lru_out(hs[0], hs[1], gs_tm, w_out0, x, gate0, g1, shift1, scale1, row_off=clen)

    lam_init = 0.8 - 0.6 * math.exp(-0.3 * 1)
    wq, wk, wv, wg = (att_w_in[0][:, k * aw:(k + 1) * aw].astype(BF16) for k in range(4))
    gmat = jnp.kron(jnp.eye(MXU_DIM // HEAD_DIM, dtype=F32),
                    jnp.full((HEAD_DIM, HEAD_DIM), 1.0 / HEAD_DIM, F32)).astype(BF16)
    reps = aw // HEAD_DIM
    q_gain = jnp.tile(att_q_norm[0] * (HEAD_DIM ** -0.5 * math.log2(math.e)), reps)[None, :]
    k_gain = jnp.tile(att_k_norm[0], reps)[None, :]
    tabs = _rope_tables(s)
    qh, = _att_proj(h1_lat, wq, mode="q", gain=q_gain, gmat=gmat, rope_tabs=tabs)
    k_lat, = _att_proj(h1_lat, wk, mode="k", gain=k_gain, gmat=gmat, rope_tabs=tabs)
    k_ctx, = _att_proj(h1_ctx, wk, mode="k", gain=k_gain, gmat=gmat)
    vt_lat, = _att_proj(h1_lat, wv.T, mode="v")
    vt_ctx, = _att_proj(h1_ctx, wv.T, mode="v")
    gsa, = _att_proj(h1_lat, wg, mode="g")
    subln_t = jnp.broadcast_to((att_subln[0] * (1.0 - lam_init))[:, None], (LANES, min(ATT_TQ, s)))
    o_heads = _attention(qh, k_lat, k_ctx, vt_lat, vt_ctx, att_lambda[0], subln_t, lam_init=lam_init)
    return _att_out(o_heads, gsa, att_w_out[0].astype(BF16), x1, gate1)
```

```python
import functools
import math

import jax
import jax.numpy as jnp
from jax import lax
from jax.experimental import pallas as pl
from jax.experimental.pallas import tpu as pltpu

F32 = jnp.float32
BF16 = jnp.bfloat16

EPS = 1e-6
LRU_C = 8.0
GRID_W = 64
HEAD_DIM = 64
ROPE_FREQS = HEAD_DIM // 4
ROPE_BASE = 10000.0
LANES = 128
MXU_DIM = 256
VMEM_LIMIT = 56 * 1024 * 1024
V_ROWS = LANES + 16
ATT_TQ = 1024
LRU_TT = 32
LRU_SUB = 1
_NT = (((1,), (1,)), ((), ()))


def _params(sem, vmem=VMEM_LIMIT):
    return pltpu.CompilerParams(dimension_semantics=sem, vmem_limit_bytes=vmem)


def _sigmoid(x):
    return 1.0 / (1.0 + jnp.exp(-x))


def _silu(x):
    return x * _sigmoid(x)


def _dot(a, b):
    return jnp.dot(a, b, preferred_element_type=F32)


def _resident(shape):
    nd = len(shape)
    return pl.BlockSpec(shape, lambda *_: (0,) * nd, pipeline_mode=pl.Buffered(1))


def _mod_kernel(cs_ref, w_ref, b_ref, o_ref):
    a = _silu(cs_ref[...])
    w = w_ref[0]
    a_hi = a.astype(BF16)
    a_lo = (a - a_hi.astype(F32)).astype(BF16)
    w_hi = w.astype(BF16)
    w_lo = (w - w_hi.astype(F32)).astype(BF16)
    acc = _dot(a_hi, w_hi) + _dot(a_lo, w_hi) + _dot(a_hi, w_lo)
    o_ref[0] = acc + b_ref[0]


def _modulation(cs, mod_w, mod_b):
    depth, d, n3 = mod_w.shape
    rows = cs.shape[0]
    tn = 768 if n3 % 768 == 0 else n3
    return pl.pallas_call(
        _mod_kernel,
        grid=(depth, n3 // tn),
        in_specs=[
            pl.BlockSpec((rows, d), lambda i, n: (0, 0)),
            pl.BlockSpec((1, d, tn), lambda i, n: (i, 0, n)),
            pl.BlockSpec((1, 1, tn), lambda i, n: (i, 0, n)),
        ],
        out_specs=pl.BlockSpec((1, rows, tn), lambda i, n: (i, 0, n)),
        out_shape=jax.ShapeDtypeStruct((depth, rows, n3), F32),
        compiler_params=_params(("parallel", "parallel")),
        name="modulation",
    )(cs, mod_w, mod_b.reshape(depth, 1, n3))


def _norm_mod(xb, g, scale, shift):
    ms = jnp.mean(xb * xb, axis=-1, keepdims=True)
    return (xb * lax.rsqrt(ms + EPS) * g) * (1.0 + scale) + shift


def _slab_pitch(nb):
    return nb + 8


def _lru_in_kernel(ctx_ref, x_ref, sh_ref, sc_ref, g_ref, w_ref, u_ref, gs_ref, *slabs, nc, nb, tt, d, w, nch):
    is_ctx = pl.program_id(0) < nc
    m = tt * nb
    pitch = _slab_pitch(nb)
    g = g_ref[...]
    for k, slab_ref in enumerate(slabs):
        for b in range(nb):
            xb = jnp.where(is_ctx, ctx_ref[b, k * tt:(k + 1) * tt, :], x_ref[b, k * tt:(k + 1) * tt, :])
            h = _norm_mod(xb, g, sc_ref[0, b:b + 1, :], sh_ref[0, b:b + 1, :])
            for s in range(d // LANES):
                slab_ref[s, pl.ds(b, tt, stride=pitch), :] = h[:, s * LANES:(s + 1) * LANES]
        hfull = jnp.concatenate(
            [jnp.concatenate([slab_ref[s, t * pitch:t * pitch + nb, :] for t in range(tt)], axis=0)
             for s in range(d // LANES)], axis=1).astype(BF16)
        rows = slice(k * m, (k + 1) * m)
        for n in range(0, 2 * w, nch):
            res = _dot(hfull, w_ref[:, n:n + nch])
            if n < w:
                u_ref[rows, n:n + nch] = res.astype(BF16)
            else:
                gs_ref[rows, n - w:n - w + nch] = _silu(res).astype(BF16)


def _lru_in(ctx, x, shift2, scale2, g, w_in):
    nb, c, d = ctx.shape
    s = x.shape[1]
    w = w_in.shape[1] // 2
    tt, sub = LRU_TT, LRU_SUB
    step = sub * tt
    assert c % step == 0 and s % step == 0
    nc = c // step
    m = tt * nb
    nch = min(512, w)
    kern = functools.partial(_lru_in_kernel, nc=nc, nb=nb, tt=tt, d=d, w=w, nch=nch)
    mod = pl.BlockSpec((1, nb, d), lambda j: (jnp.where(j < nc, 0, 1), 0, 0))
    out = jax.ShapeDtypeStruct(((c + s) * nb, w), BF16)
    tiles = pl.BlockSpec((sub * m, w), lambda j: (j, 0))
    return pl.pallas_call(
        kern,
        grid=((c + s) // step,),
        in_specs=[
            pl.BlockSpec((nb, step, d), lambda j: (0, jnp.minimum(j, nc - 1), 0)),
            pl.BlockSpec((nb, step, d), lambda j: (0, jnp.maximum(j - nc, 0), 0)),
            mod, mod, pl.BlockSpec((1, d), lambda j: (0, 0)),
            _resident((d, 2 * w)),
        ],
        out_specs=[tiles, tiles],
        out_shape=[out, out],
        scratch_shapes=[pltpu.VMEM((d // LANES, tt * _slab_pitch(nb), LANES), F32)] * sub,
        compiler_params=_params(("parallel",)),
        name="lru_in",
    )(ctx, x, shift2, scale2, g, w_in)


def _scan_chunk(j, nc, ntot, reverse):
    if not reverse:
        return j
    return jnp.where(j < nc, nc - 1 - j, ntot - 1 - j + nc)


def _lru_scan_kernel(prev_ref, cur_ref, next_ref, cw_ref, cb_ref, gw_ref, lam_ref, hs_ref,
                     a_ref, b_ref, h_ref, *, nc, ntot, nb, tc, wb, reverse):
    j = pl.program_id(1)
    c = _scan_chunk(j, nc, ntot, reverse)
    seq_start = (c == 0) | (c == nc)
    seq_end = (c == nc - 1) | (c == ntot - 1)
    m = tc * nb

    @pl.when(j == 0)
    def _():
        h_ref[...] = jnp.zeros_like(h_ref)

    prev = prev_ref[...]
    nxt = next_ref[0:nb, :]
    ext = jnp.concatenate([jnp.where(seq_start, jnp.zeros_like(prev), prev), cur_ref[...],
                           jnp.where(seq_end, jnp.zeros_like(nxt), nxt)], axis=0)
    cw = cw_ref[...].astype(BF16)
    xc = cb_ref[...].astype(BF16) + cw[0:1] * ext[0:m]
    for k in range(1, 4):
        xc = xc + cw[k:k + 1] * ext[k * nb:k * nb + m]

    lam = lam_ref[...]
    half_c = (0.5 * LRU_C) * (jnp.maximum(-lam, 0.0) + jnp.log1p(jnp.exp(-jnp.abs(lam))))
    nslab = wb // LANES
    ones = jnp.ones((m, LANES), BF16)
    for s in range(nslab):
        xsb = xc[:, s * LANES:(s + 1) * LANES]
        xs = xsb.astype(F32)
        th = jnp.tanh(_dot(jnp.concatenate([xsb, ones], axis=1), gw_ref[s]))
        hc = half_c[:, s * LANES:(s + 1) * LANES]
        nla = th[:, :LANES] * hc + hc
        a = jnp.exp2(nla * (-math.log2(math.e)))
        a_ref[s] = a
        y = jnp.tanh(nla) * (a * a + 1.0)
        sq = jnp.where(y > 0.0, y * lax.rsqrt(y), 0.0)
        b_ref[s] = sq * ((0.5 * th[:, LANES:] + 0.5) * xs)

    def step(i, hs):
        t = (tc - 1 - i) if reverse else i
        rows = pl.ds(pl.multiple_of(t * nb, nb), nb)
        out = []
        for s in range(nslab):
            h = a_ref[s, rows, :] * hs[s] + b_ref[s, rows, :]
            hs_ref[rows, s * LANES:(s + 1) * LANES] = h.astype(BF16)
            out.append(h)
        return tuple(out)

    hs = lax.fori_loop(0, tc, step, tuple(h_ref[s] for s in range(nslab)), unroll=4)
    for s in range(nslab):
        h_ref[s] = hs[s]


def _lru_scan(u_tm, conv_w, conv_b, gw, lam, *, nb, c, reverse):
    rows, w = u_tm.shape
    ltot = rows // nb
    tc = math.gcd(c, 64)
    wb = min(1024, w)
    nc, ntot = c // tc, ltot // tc
    m = tc * nb
    hb = tc // 2
    last_halo = ltot // 2 - 1
    chunk = functools.partial(_scan_chunk, nc=nc, ntot=ntot, reverse=reverse)
    kern = functools.partial(_lru_scan_kernel, nc=nc, ntot=ntot, nb=nb, tc=tc, wb=wb, reverse=reverse)
    nslab = wb // LANES
    return pl.pallas_call(
        kern,
        grid=(w // wb, ntot),
        in_specs=[
            pl.BlockSpec((2 * nb, wb), lambda i, j: (jnp.maximum(chunk(j) * hb - 1, 0), i)),
            pl.BlockSpec((m, wb), lambda i, j: (chunk(j), i)),
            pl.BlockSpec((2 * nb, wb), lambda i, j: (jnp.minimum((chunk(j) + 1) * hb, last_halo), i)),
            pl.BlockSpec((4, wb), lambda i, j: (0, i)),
            pl.BlockSpec((1, wb), lambda i, j: (0, i)),
            pl.BlockSpec((nslab, 2 * LANES, 2 * LANES), lambda i, j: (i, 0, 0)),
            pl.BlockSpec((1, wb), lambda i, j: (0, i)),
        ],
        out_specs=pl.BlockSpec((m, wb), lambda i, j: (chunk(j), i)),
        out_shape=jax.ShapeDtypeStruct((rows, w), BF16),
        scratch_shapes=[
            pltpu.VMEM((nslab, m, LANES), F32),
            pltpu.VMEM((nslab, m, LANES), F32),
            pltpu.VMEM((nslab, nb, LANES), F32),
        ],
        compiler_params=_params(("parallel", "arbitrary")),
        name="lru_scan_bwd" if reverse else "lru_scan_fwd",
    )(u_tm, u_tm, u_tm, conv_w, conv_b, gw, lam)


def _lru_out_kernel(hf_ref, hb_ref, gs_ref, w_ref, x_ref, gate_ref, g_ref, sh_ref, sc_ref, x1_ref, h1_ref,
                    *slabs, nb, tt, d, nch):
    m = tt * nb
    per = nch // LANES
    pitch = _slab_pitch(nb)
    g = g_ref[...]
    for k, slab_ref in enumerate(slabs):
        rows = slice(k * m, (k + 1) * m)
        z = ((hf_ref[rows, :].astype(F32) + hb_ref[rows, :].astype(F32)) * gs_ref[rows, :].astype(F32)).astype(BF16)
        for c in range(d // nch):
            y = _dot(z, w_ref[:, c * nch:(c + 1) * nch])
            for j in range(per):
                for t in range(tt):
                    slab_ref[c * per + j, t * pitch:t * pitch + nb, :] = y[t * nb:(t + 1) * nb, j * LANES:(j + 1) * LANES]
        t0 = k * tt
        for b in range(nb):
            yb = jnp.concatenate([slab_ref[s, pl.ds(b, tt, stride=pitch), :] for s in range(d // LANES)], axis=1)
            x1 = x_ref[b, t0:t0 + tt, :] + gate_ref[b] * yb
            x1_ref[b, t0:t0 + tt, :] = x1
            h1_ref[b, t0:t0 + tt, :] = _norm_mod(x1, g, sc_ref[b], sh_ref[b]).astype(BF16)


def _lru_out(hs_f, hs_b, gs, w_out, xres, gate, g1, shift1, scale1, *, row_off):
    nb, l, d = xres.shape
    w = w_out.shape[0]
    tt, sub = LRU_TT, LRU_SUB
    step = sub * tt
    assert l % step == 0 and row_off % step == 0
    m = tt * nb
    off = row_off // step
    kern = functools.partial(_lru_out_kernel, nb=nb, tt=tt, d=d, nch=min(512, d))
    bmod = pl.BlockSpec((nb, 1, d), lambda j: (0, 0, 0))
    act = pl.BlockSpec((sub * m, w), lambda j: (j + off, 0))
    res = pl.BlockSpec((nb, step, d), lambda j: (0, j, 0))
    return pl.pallas_call(
        kern,
        grid=(l // step,),
        in_specs=[act, act, act, _resident((w, d)), res, bmod, pl.BlockSpec((1, d), lambda j: (0, 0)), bmod, bmod],
        out_specs=[res, res],
        out_shape=[jax.ShapeDtypeStruct((nb, l, d), F32), jax.ShapeDtypeStruct((nb, l, d), BF16)],
        scratch_shapes=[pltpu.VMEM((d // LANES, tt * _slab_pitch(nb), LANES), F32)] * sub,
        compiler_params=_params(("parallel",)),
        name="lru_out",
    )(hs_f, hs_b, gs, w_out, xres, gate, g1, shift1, scale1)


def _att_proj_kernel(*refs, mode, rope, nheads):
    refs = list(refs)
    h_ref, w_ref = refs[0], refs[1]
    pos = 2
    if mode in ("q", "k"):
        gain_ref, gmat_ref = refs[pos], refs[pos + 1]
        pos += 2
        if rope:
            cos_ref, sa_ref, sb_ref = refs[pos:pos + 3]
            pos += 3
    outs = refs[pos:]
    if mode == "v":
        res_t = lax.dot_general(w_ref[...], h_ref[0], _NT, preferred_element_type=F32)
        ones = jnp.ones((V_ROWS - LANES, res_t.shape[1]), BF16)
        for h in range(nheads):
            outs[0][0, h, 0:LANES, :] = res_t[h * LANES:(h + 1) * LANES, :].astype(BF16)
            outs[0][0, h, LANES:V_ROWS, :] = ones
        return
    res_ref = outs.pop()

    @pl.when(pl.program_id(0) == 0)
    def _():
        res_ref[...] = jnp.zeros_like(res_ref)

    for p in range(res_ref.shape[1] // MXU_DIM):
        r = res_ref[:, p * MXU_DIM:(p + 1) * MXU_DIM]
        if mode == "g":
            outs[0][0, :, p * MXU_DIM:(p + 1) * MXU_DIM] = _silu(r).astype(BF16)
            continue
        ms = _dot((r * r).astype(BF16), gmat_ref[...])
        rn = r * lax.rsqrt(ms + EPS) * gain_ref[:, p * MXU_DIM:(p + 1) * MXU_DIM]
        for hh in range(MXU_DIM // LANES):
            xh = rn[:, hh * LANES:(hh + 1) * LANES]
            if rope:
                xh = (xh * cos_ref[...] + pltpu.roll(xh, LANES - ROPE_FREQS, 1) * sa_ref[...]
                      + pltpu.roll(xh, ROPE_FREQS, 1) * sb_ref[...])
            outs[0][0, p * (MXU_DIM // LANES) + hh] = xh.astype(BF16)
    res_ref[...] = _dot(h_ref[0], w_ref[...])


def _att_proj(h1, w, *, mode, gain=None, gmat=None, rope_tabs=None):
    nb, l, d = h1.shape
    n = w.shape[0] if mode == "v" else w.shape[1]
    tr = min(1024, l)
    nheads = n // LANES
    rope = rope_tabs is not None
    nt = l // tr
    ntiles = nb * nt
    lag = 0 if mode == "v" else 1

    def cur(j):
        t = jnp.minimum(j, ntiles - 1)
        return t // nt, t % nt

    def done(j):
        t = jnp.maximum(j - lag, 0)
        return t // nt, t % nt

    const = lambda j: (0, 0)
    args = [h1, w]
    in_specs = [pl.BlockSpec((1, tr, d), lambda j: (*cur(j), 0)), _resident(w.shape)]
    scratch = [pltpu.VMEM((tr, n), F32)] if lag else []
    if mode in ("q", "k"):
        args += [gain, gmat]
        in_specs += [pl.BlockSpec((1, n), const), pl.BlockSpec((MXU_DIM, MXU_DIM), const)]
        if rope:
            args += list(rope_tabs)
            in_specs += [pl.BlockSpec((tr, LANES), lambda j: (done(j)[1], 0))] * 3
    if mode == "g":
        out_specs = [pl.BlockSpec((1, tr, n), lambda j: (*done(j), 0))]
        out_shape = [jax.ShapeDtypeStruct((nb, l, n), BF16)]
    elif mode == "v":
        out_specs = [pl.BlockSpec((1, nheads, V_ROWS, tr), lambda j: (done(j)[0], 0, 0, done(j)[1]))]
        out_shape = [jax.ShapeDtypeStruct((nb, nheads, V_ROWS, l), BF16)]
    else:
        hspec = pl.BlockSpec((1, nheads, tr, LANES), lambda j: (done(j)[0], 0, done(j)[1], 0))
        hshape = jax.ShapeDtypeStruct((nb, nheads, l, LANES), BF16)
        out_specs, out_shape = [hspec], [hshape]
    kern = functools.partial(_att_proj_kernel, mode=mode, rope=rope, nheads=nheads)
    return pl.pallas_call(
        kern,
        grid=(ntiles + lag,),
        in_specs=in_specs,
        out_specs=out_specs,
        out_shape=out_shape,
        scratch_shapes=scratch,
        compiler_params=_params(("arbitrary",)),
        name="att_proj_" + mode + ("_rope" if rope else ""),
    )(*args)


def _attn_kernel(q_ref, kl_ref, kc_ref, vtl_ref, vtc_ref, lv_ref, sg_ref, o_ref, st_ref, m_ref, p_ref, *,
                 lam_init, tq):
    @pl.when(pl.program_id(0) == 0)
    def _():
        st_ref[...] = jnp.zeros_like(st_ref)
        m_ref[...] = jnp.zeros_like(m_ref)
        p_ref[...] = jnp.ones_like(p_ref)

    lv = lv_ref[...]
    lam = (jnp.exp(jnp.sum(lv[0:1] * lv[1:2], axis=-1, keepdims=True))
           - jnp.exp(jnp.sum(lv[2:3] * lv[3:4], axis=-1, keepdims=True)) + lam_init)
    s = kl_ref.shape[2]
    ot = _dot(vtl_ref[0, 0], p_ref[0:s, :]) + _dot(vtc_ref[0, 0], p_ref[s:, :])
    on = ot[0:LANES] * (1.0 / ot[LANES:LANES + 1])
    od = on[:, 0:tq] - lam * on[:, tq:2 * tq]
    ms = jnp.mean(od * od, axis=0, keepdims=True)
    o_ref[0, 0] = (od * lax.rsqrt(ms + EPS) * sg_ref[...]).T.astype(BF16)

    p_ref[...] = jnp.exp2((st_ref[...] - m_ref[...]).astype(BF16))

    qh = q_ref[0, 0]
    first = lax.broadcasted_iota(jnp.int32, qh.shape, 1) < HEAD_DIM
    zero = jnp.zeros_like(qh)
    q = jnp.concatenate([jnp.where(first, qh, zero), jnp.where(first, zero, qh)], axis=0)
    st_l = lax.dot_general(kl_ref[0, 0], q, _NT, preferred_element_type=F32)
    st_c = lax.dot_general(kc_ref[0, 0], q, _NT, preferred_element_type=F32)
    st_ref[0:s, :] = st_l
    st_ref[s:, :] = st_c
    m_ref[...] = jnp.maximum(jnp.max(st_l, axis=0, keepdims=True), jnp.max(st_c, axis=0, keepdims=True))


def _attention(qh, k_lat, k_ctx, vt_lat, vt_ctx, lam_vecs, subln_t, *, lam_init):
    nb, nh, s, _ = qh.shape
    clen = k_ctx.shape[2]
    ltot = s + clen
    tq = subln_t.shape[1]
    nq = s // tq
    ntiles = nb * nh * nq

    def tile(t):
        return t // (nh * nq), (t // nq) % nh, t % nq

    def q_map(i):
        b, h, qi = tile(jnp.minimum(i, ntiles - 1))
        return b, h, qi, 0

    def k_map(i):
        b, h, _ = tile(jnp.minimum(i, ntiles - 1))
        return b, h, 0, 0

    def v_map(i):
        b, h, _ = tile(jnp.maximum(i - 2, 0))
        return b, h, 0, 0

    def o_map(i):
        b, h, qi = tile(jnp.maximum(i - 2, 0))
        return b, h, qi, 0

    qspec = pl.BlockSpec((1, 1, tq, LANES), q_map)
    return pl.pallas_call(
        functools.partial(_attn_kernel, lam_init=lam_init, tq=tq),
        grid=(ntiles + 2,),
        in_specs=[qspec,
                  pl.BlockSpec((1, 1, s, LANES), k_map),
                  pl.BlockSpec((1, 1, clen, LANES), k_map),
                  pl.BlockSpec((1, 1, V_ROWS, s), v_map),
                  pl.BlockSpec((1, 1, V_ROWS, clen), v_map),
                  pl.BlockSpec((4, HEAD_DIM), lambda i: (0, 0)),
                  pl.BlockSpec((LANES, tq), lambda i: (0, 0))],
        out_specs=pl.BlockSpec((1, 1, tq, LANES), o_map),
        out_shape=jax.ShapeDtypeStruct((nb, nh, s, LANES), BF16),
        scratch_shapes=[pltpu.VMEM((ltot, 2 * tq), F32), pltpu.VMEM((1, 2 * tq), F32),
                        pltpu.VMEM((ltot, 2 * tq), BF16)],
        compiler_params=_params(("arbitrary",)),
        name="diff_attention",
    )(qh, k_lat, k_ctx, vt_lat, vt_ctx, lam_vecs, subln_t)


def _att_out_kernel(o_ref, gs_ref, w_ref, x_ref, gate_ref, out_ref, *, nheads):
    o = jnp.concatenate([o_ref[0, h] for h in range(nheads)], axis=1)
    z = (o.astype(F32) * gs_ref[0].astype(F32)).astype(BF16)
    out_ref[0] = x_ref[0] + gate_ref[0] * _dot(z, w_ref[...])


def _att_out(o_heads, gs, w_out, x1, gate):
    nb, nh, s, _ = o_heads.shape
    d = w_out.shape[1]
    wd = w_out.shape[0]
    tr = min(512, s)
    return pl.pallas_call(
        functools.partial(_att_out_kernel, nheads=nh),
        grid=(nb, s // tr),
        in_specs=[
            pl.BlockSpec((1, nh, tr, LANES), lambda b, i: (b, 0, i, 0)),
            pl.BlockSpec((1, tr, wd), lambda b, i: (b, i, 0)),
            _resident((wd, d)),
            pl.BlockSpec((1, tr, d), lambda b, i: (b, i, 0)),
            pl.BlockSpec((1, 1, d), lambda b, i: (b, 0, 0)),
        ],
        out_specs=pl.BlockSpec((1, tr, d), lambda b, i: (b, i, 0)),
        out_shape=jax.ShapeDtypeStruct((nb, s, d), F32),
        compiler_params=_params(("parallel", "parallel")),
        name="att_out",
    )(o_heads, gs, w_out, x1, gate)


def _gate_weights(gw, gb):
    w = 0.5 * jnp.concatenate([gw[0], gw[1]], axis=-1)
    b = 0.5 * jnp.concatenate([gb[0], gb[1]], axis=-1)
    b_hi = b.astype(BF16)
    b_lo = (b - b_hi.astype(F32)).astype(BF16)
    pad = jnp.zeros((w.shape[0], LANES - 2, 2 * LANES), BF16)
    return jnp.concatenate([w.astype(BF16), b_hi[:, None, :], b_lo[:, None, :], pad], axis=1)


def _rope_tables(s):
    t = jnp.arange(s)
    pos = jnp.stack([(t // GRID_W).astype(F32), (t % GRID_W).astype(F32)], axis=1)
    inv = ROPE_BASE ** (-jnp.arange(ROPE_FREQS, dtype=F32) / ROPE_FREQS)
    ang = pos[:, :, None] * inv
    lane = jnp.arange(LANES)
    dd = lane % HEAD_DIM
    axis, half, f = dd // (2 * ROPE_FREQS), (dd % (2 * ROPE_FREQS)) // ROPE_FREQS, dd % ROPE_FREQS
    cos = jnp.cos(ang)[:, axis, f]
    sin = jnp.sin(ang)[:, axis, f]
    return cos, jnp.where(half == 0, -sin, 0.0), jnp.where(half == 1, sin, 0.0)


def kernel(x, c, ctx, c_ctx, mod_w, mod_b, norm_g, lru_w_in, lru_conv_w, lru_conv_b, lru_gate_w, lru_gate_b,
           lru_lambda, lru_w_out, att_w_in, att_q_norm, att_k_norm, att_lambda, att_subln, att_w_out):
    nb, s, d = x.shape
    clen = ctx.shape[1]
    aw = att_w_out.shape[1]
    assert lru_gate_w.shape[-1] == LANES and d % MXU_DIM == 0 and nb % 8 == 0

    rows = ((nb + 1 + 7) // 8) * 8
    cs = jnp.zeros((rows, d), F32).at[:nb].set(c).at[nb].set(c_ctx)
    mod = _modulation(cs, mod_w, mod_b)

    def split(i):
        lat = [mod[i, :nb, k * d:(k + 1) * d].reshape(nb, 1, d) for k in range(3)]
        cx = [jnp.broadcast_to(mod[i, nb:nb + 1, k * d:(k + 1) * d], (nb, d)).reshape(nb, 1, d) for k in range(3)]
        return lat, cx

    (shift0, scale0, gate0), (cshift0, cscale0, cgate0) = split(0)
    (shift1, scale1, gate1), (cshift1, cscale1, _) = split(1)

    w_in0 = lru_w_in[0].astype(BF16)
    ltot = s + clen
    both = lambda a, b: jnp.stack([a.reshape(nb, d), b.reshape(nb, d)])
    u_tm, gs_tm = _lru_in(ctx, x, both(cshift0, shift0), both(cscale0, scale0), norm_g[0:1], w_in0)
    hs = []
    for dr in range(2):
        hs.append(_lru_scan(u_tm, lru_conv_w[0], lru_conv_b[0:1], _gate_weights(lru_gate_w[0, dr], lru_gate_b[0, dr]),
                            lru_lambda[0, dr:dr + 1], nb=nb, c=clen, reverse=bool(dr)))
    w_out0 = lru_w_out[0].astype(BF16)
    g1 = norm_g[1:2]
    _, h1_ctx = _lru_out(hs[0], hs[1], gs_tm, w_out0, ctx, cgate0, g1, cshift1, cscale1, row_off=0)
    x1, h1_lat = _lru_out(hs[0], hs[1], gs_tm, w_out0, x, gate0, g1, shift1, scale1, row_off=clen)

    lam_init = 0.8 - 0.6 * math.exp(-0.3 * 1)
    wq, wk, wv, wg = (att_w_in[0][:, k * aw:(k + 1) * aw].astype(BF16) for k in range(4))
    gmat = jnp.kron(jnp.eye(MXU_DIM // HEAD_DIM, dtype=F32),
                    jnp.full((HEAD_DIM, HEAD_DIM), 1.0 / HEAD_DIM, F32)).astype(BF16)
    reps = aw // HEAD_DIM
    q_gain = jnp.tile(att_q_norm[0] * (HEAD_DIM ** -0.5 * math.log2(math.e)), reps)[None, :]
    k_gain = jnp.tile(att_k_norm[0], reps)[None, :]
    tabs = _rope_tables(s)
    qh, = _att_proj(h1_lat, wq, mode="q", gain=q_gain, gmat=gmat, rope_tabs=tabs)
    k_lat, = _att_proj(h1_lat, wk, mode="k", gain=k_gain, gmat=gmat, rope_tabs=tabs)
    k_ctx, = _att_proj(h1_ctx, wk, mode="k", gain=k_gain, gmat=gmat)
    vt_lat, = _att_proj(h1_lat, wv.T, mode="v")
    vt_ctx, = _att_proj(h1_ctx, wv.T, mode="v")
    gsa, = _att_proj(h1_lat, wg, mode="g")
    subln_t = jnp.broadcast_to((att_subln[0] * (1.0 - lam_init))[:, None], (LANES, min(ATT_TQ, s)))
    o_heads = _attention(qh, k_lat, k_ctx, vt_lat, vt_ctx, att_lambda[0], subln_t, lam_init=lam_init)
    return _att_out(o_heads, gsa, att_w_out[0].astype(BF16), x1, gate1)
```

```python
import functools
import math

import jax
import jax.numpy as jnp
from jax import lax
from jax.experimental import pallas as pl
from jax.experimental.pallas import tpu as pltpu

F32 = jnp.float32
BF16 = jnp.bfloat16

EPS = 1e-6
LRU_C = 8.0
GRID_W = 64
HEAD_DIM = 64
ROPE_FREQS = HEAD_DIM // 4
ROPE_BASE = 10000.0
LANES = 128
MXU_DIM = 256
VMEM_LIMIT = 56 * 1024 * 1024
V_ROWS = LANES + 16
ATT_TQ = 1024
LRU_TT = 32
LRU_SUB = 1
_NT = (((1,), (1,)), ((), ()))


def _params(sem, vmem=VMEM_LIMIT):
    return pltpu.CompilerParams(dimension_semantics=sem, vmem_limit_bytes=vmem)


def _sigmoid(x):
    return 1.0 / (1.0 + jnp.exp(-x))


def _silu(x):
    return x * _sigmoid(x)


def _dot(a, b):
    return jnp.dot(a, b, preferred_element_type=F32)


def _resident(shape):
    nd = len(shape)
    return pl.BlockSpec(shape, lambda *_: (0,) * nd, pipeline_mode=pl.Buffered(1))


def _mod_kernel(cs_ref, w_ref, b_ref, o_ref):
    a = _silu(cs_ref[...])
    w = w_ref[0]
    a_hi = a.astype(BF16)
    a_lo = (a - a_hi.astype(F32)).astype(BF16)
    w_hi = w.astype(BF16)
    w_lo = (w - w_hi.astype(F32)).astype(BF16)
    acc = _dot(a_hi, w_hi) + _dot(a_lo, w_hi) + _dot(a_hi, w_lo)
    o_ref[0] = acc + b_ref[0]


def _modulation(cs, mod_w, mod_b):
    depth, d, n3 = mod_w.shape
    rows = cs.shape[0]
    tn = 768 if n3 % 768 == 0 else n3
    return pl.pallas_call(
        _mod_kernel,
        grid=(depth, n3 // tn),
        in_specs=[
            pl.BlockSpec((rows, d), lambda i, n: (0, 0)),
            pl.BlockSpec((1, d, tn), lambda i, n: (i, 0, n)),
            pl.BlockSpec((1, 1, tn), lambda i, n: (i, 0, n)),
        ],
        out_specs=pl.BlockSpec((1, rows, tn), lambda i, n: (i, 0, n)),
        out_shape=jax.ShapeDtypeStruct((depth, rows, n3), F32),
        compiler_params=_params(("parallel", "parallel")),
        name="modulation",
    )(cs, mod_w, mod_b.reshape(depth, 1, n3))


def _norm_mod(xb, g, scale, shift):
    ms = jnp.mean(xb * xb, axis=-1, keepdims=True)
    return (xb * lax.rsqrt(ms + EPS) * g) * (1.0 + scale) + shift


def _slab_pitch(nb):
    return nb + 8


def _lru_in_kernel(ctx_ref, x_ref, sh_ref, sc_ref, g_ref, w_ref, u_ref, gs_ref, *slabs, nc, nb, tt, d, w, nch):
    is_ctx = pl.program_id(0) < nc
    m = tt * nb
    pitch = _slab_pitch(nb)
    g = g_ref[...]
    for k, slab_ref in enumerate(slabs):
        for b in range(nb):
            xb = jnp.where(is_ctx, ctx_ref[b, k * tt:(k + 1) * tt, :], x_ref[b, k * tt:(k + 1) * tt, :])
            h = _norm_mod(xb, g, sc_ref[0, b:b + 1, :], sh_ref[0, b:b + 1, :])
            for s in range(d // LANES):
                slab_ref[s, pl.ds(b, tt, stride=pitch), :] = h[:, s * LANES:(s + 1) * LANES]
        hfull = jnp.concatenate(
            [jnp.concatenate([slab_ref[s, t * pitch:t * pitch + nb, :] for t in range(tt)], axis=0)
             for s in range(d // LANES)], axis=1).astype(BF16)
        rows = slice(k * m, (k + 1) * m)
        for n in range(0, 2 * w, nch):
            res = _dot(hfull, w_ref[:, n:n + nch])
            if n < w:
                u_ref[rows, n:n + nch] = res.astype(BF16)
            else:
                gs_ref[rows, n - w:n - w + nch] = _silu(res).astype(BF16)


def _lru_in(ctx, x, shift2, scale2, g, w_in):
    nb, c, d = ctx.shape
    s = x.shape[1]
    w = w_in.shape[1] // 2
    tt, sub = LRU_TT, LRU_SUB
    step = sub * tt
    assert c % step == 0 and s % step == 0
    nc = c // step
    m = tt * nb
    nch = min(512, w)
    kern = functools.partial(_lru_in_kernel, nc=nc, nb=nb, tt=tt, d=d, w=w, nch=nch)
    mod = pl.BlockSpec((1, nb, d), lambda j: (jnp.where(j < nc, 0, 1), 0, 0))
    out = jax.ShapeDtypeStruct(((c + s) * nb, w), BF16)
    tiles = pl.BlockSpec((sub * m, w), lambda j: (j, 0))
    return pl.pallas_call(
        kern,
        grid=((c + s) // step,),
        in_specs=[
            pl.BlockSpec((nb, step, d), lambda j: (0, jnp.minimum(j, nc - 1), 0)),
            pl.BlockSpec((nb, step, d), lambda j: (0, jnp.maximum(j - nc, 0), 0)),
            mod, mod, pl.BlockSpec((1, d), lambda j: (0, 0)),
            _resident((d, 2 * w)),
        ],
        out_specs=[tiles, tiles],
        out_shape=[out, out],
        scratch_shapes=[pltpu.VMEM((d // LANES, tt * _slab_pitch(nb), LANES), F32)] * sub,
        compiler_params=_params(("parallel",)),
        name="lru_in",
    )(ctx, x, shift2, scale2, g, w_in)


def _scan_chunk(j, nc, ntot, reverse):
    if not reverse:
        return j
    return jnp.where(j < nc, nc - 1 - j, ntot - 1 - j + nc)


def _lru_scan_kernel(*refs, nc, ntot, nb, tc, wb, reverse, from_xc):
    a_ref, b_ref, h_ref = refs[-3:]
    j = pl.program_id(1)
    c = _scan_chunk(j, nc, ntot, reverse)
    m = tc * nb

    @pl.when(j == 0)
    def _():
        h_ref[...] = jnp.zeros_like(h_ref)

    if from_xc:
        xc_in_ref, gw_ref, lam_ref, hs_ref = refs[:4]
        xc = xc_in_ref[...]
    else:
        prev_ref, cur_ref, next_ref, cw_ref, cb_ref, gw_ref, lam_ref, hs_ref, xc_out_ref = refs[:9]
        seq_start = (c == 0) | (c == nc)
        seq_end = (c == nc - 1) | (c == ntot - 1)
        prev = prev_ref[...]
        nxt = next_ref[0:nb, :]
        ext = jnp.concatenate([jnp.where(seq_start, jnp.zeros_like(prev), prev), cur_ref[...],
                               jnp.where(seq_end, jnp.zeros_like(nxt), nxt)], axis=0)
        cw = cw_ref[...].astype(BF16)
        xc = cb_ref[...].astype(BF16) + cw[0:1] * ext[0:m]
        for k in range(1, 4):
            xc = xc + cw[k:k + 1] * ext[k * nb:k * nb + m]
        xc_out_ref[...] = xc

    lam = lam_ref[...]
    half_c = (0.5 * LRU_C) * (jnp.maximum(-lam, 0.0) + jnp.log1p(jnp.exp(-jnp.abs(lam))))
    nslab = wb // LANES
    ones = jnp.ones((m, LANES), BF16)
    for s in range(nslab):
        xsb = xc[:, s * LANES:(s + 1) * LANES]
        xs = xsb.astype(F32)
        th = jnp.tanh(_dot(jnp.concatenate([xsb, ones], axis=1), gw_ref[s]))
        hc = half_c[:, s * LANES:(s + 1) * LANES]
        nla = th[:, :LANES] * hc + hc
        a = jnp.exp2(nla * (-math.log2(math.e)))
        a_ref[s] = a
        y = jnp.tanh(nla) * (a * a + 1.0)
        sq = jnp.where(y > 0.0, y * lax.rsqrt(y), 0.0)
        b_ref[s] = sq * ((0.5 * th[:, LANES:] + 0.5) * xs)

    def step(i, hs):
        t = (tc - 1 - i) if reverse else i
        rows = pl.ds(pl.multiple_of(t * nb, nb), nb)
        out = []
        for s in range(nslab):
            h = a_ref[s, rows, :] * hs[s] + b_ref[s, rows, :]
            hs_ref[rows, s * LANES:(s + 1) * LANES] = h.astype(BF16)
            out.append(h)
        return tuple(out)

    hs = lax.fori_loop(0, tc, step, tuple(h_ref[s] for s in range(nslab)), unroll=4)
    for s in range(nslab):
        h_ref[s] = hs[s]


def _lru_scan(u_tm, conv_w, conv_b, gw, lam, *, nb, c, reverse, xc=None):
    rows, w = u_tm.shape
    ltot = rows // nb
    tc = math.gcd(c, 64)
    wb = min(1024, w)
    nc, ntot = c // tc, ltot // tc
    m = tc * nb
    hb = tc // 2
    last_halo = ltot // 2 - 1
    chunk = functools.partial(_scan_chunk, nc=nc, ntot=ntot, reverse=reverse)
    from_xc = xc is not None
    kern = functools.partial(_lru_scan_kernel, nc=nc, ntot=ntot, nb=nb, tc=tc, wb=wb, reverse=reverse, from_xc=from_xc)
    nslab = wb // LANES
    cur = pl.BlockSpec((m, wb), lambda i, j: (chunk(j), i))
    chan = pl.BlockSpec((1, wb), lambda i, j: (0, i))
    gates = pl.BlockSpec((nslab, 2 * LANES, 2 * LANES), lambda i, j: (i, 0, 0))
    act = jax.ShapeDtypeStruct((rows, w), BF16)
    if from_xc:
        args, in_specs = [xc, gw, lam], [cur, gates, chan]
        out_specs, out_shape = cur, act
    else:
        args = [u_tm, u_tm, u_tm, conv_w, conv_b, gw, lam]
        in_specs = [
            pl.BlockSpec((2 * nb, wb), lambda i, j: (jnp.maximum(chunk(j) * hb - 1, 0), i)),
            cur,
            pl.BlockSpec((2 * nb, wb), lambda i, j: (jnp.minimum((chunk(j) + 1) * hb, last_halo), i)),
            pl.BlockSpec((4, wb), lambda i, j: (0, i)), chan, gates, chan,
        ]
        out_specs, out_shape = [cur, cur], [act, act]
    return pl.pallas_call(
        kern,
        grid=(w // wb, ntot),
        in_specs=in_specs,
        out_specs=out_specs,
        out_shape=out_shape,
        scratch_shapes=[
            pltpu.VMEM((nslab, m, LANES), F32),
            pltpu.VMEM((nslab, m, LANES), F32),
            pltpu.VMEM((nslab, nb, LANES), F32),
        ],
        compiler_params=_params(("parallel", "arbitrary")),
        name="lru_scan_bwd" if reverse else "lru_scan_fwd",
    )(*args)


def _lru_out_kernel(hf_ref, hb_ref, gs_ref, w_ref, x_ref, gate_ref, g_ref, sh_ref, sc_ref, x1_ref, h1_ref,
                    *slabs, nb, tt, d, nch):
    m = tt * nb
    per = nch // LANES
    pitch = _slab_pitch(nb)
    g = g_ref[...]
    for k, slab_ref in enumerate(slabs):
        rows = slice(k * m, (k + 1) * m)
        z = ((hf_ref[rows, :].astype(F32) + hb_ref[rows, :].astype(F32)) * gs_ref[rows, :].astype(F32)).astype(BF16)
        for c in range(d // nch):
            y = _dot(z, w_ref[:, c * nch:(c + 1) * nch])
            for j in range(per):
                for t in range(tt):
                    slab_ref[c * per + j, t * pitch:t * pitch + nb, :] = y[t * nb:(t + 1) * nb, j * LANES:(j + 1) * LANES]
        t0 = k * tt
        for b in range(nb):
            yb = jnp.concatenate([slab_ref[s, pl.ds(b, tt, stride=pitch), :] for s in range(d // LANES)], axis=1)
            x1 = x_ref[b, t0:t0 + tt, :] + gate_ref[b] * yb
            x1_ref[b, t0:t0 + tt, :] = x1
            h1_ref[b, t0:t0 + tt, :] = _norm_mod(x1, g, sc_ref[b], sh_ref[b]).astype(BF16)


def _lru_out(hs_f, hs_b, gs, w_out, xres, gate, g1, shift1, scale1, *, row_off):
    nb, l, d = xres.shape
    w = w_out.shape[0]
    tt, sub = LRU_TT, LRU_SUB
    step = sub * tt
    assert l % step == 0 and row_off % step == 0
    m = tt * nb
    off = row_off // step
    kern = functools.partial(_lru_out_kernel, nb=nb, tt=tt, d=d, nch=min(512, d))
    bmod = pl.BlockSpec((nb, 1, d), lambda j: (0, 0, 0))
    act = pl.BlockSpec((sub * m, w), lambda j: (j + off, 0))
    res = pl.BlockSpec((nb, step, d), lambda j: (0, j, 0))
    return pl.pallas_call(
        kern,
        grid=(l // step,),
        in_specs=[act, act, act, _resident((w, d)), res, bmod, pl.BlockSpec((1, d), lambda j: (0, 0)), bmod, bmod],
        out_specs=[res, res],
        out_shape=[jax.ShapeDtypeStruct((nb, l, d), F32), jax.ShapeDtypeStruct((nb, l, d), BF16)],
        scratch_shapes=[pltpu.VMEM((d // LANES, tt * _slab_pitch(nb), LANES), F32)] * sub,
        compiler_params=_params(("parallel",)),
        name="lru_out",
    )(hs_f, hs_b, gs, w_out, xres, gate, g1, shift1, scale1)


def _att_proj_kernel(*refs, mode, rope, nheads):
    refs = list(refs)
    h_ref, w_ref = refs[0], refs[1]
    pos = 2
    if mode in ("q", "k"):
        gain_ref, gmat_ref = refs[pos], refs[pos + 1]
        pos += 2
        if rope:
            cos_ref, sa_ref, sb_ref = refs[pos:pos + 3]
            pos += 3
    outs = refs[pos:]
    if mode == "v":
        res_t = lax.dot_general(w_ref[...], h_ref[0], _NT, preferred_element_type=F32)
        ones = jnp.ones((V_ROWS - LANES, res_t.shape[1]), BF16)
        for h in range(nheads):
            outs[0][0, h, 0:LANES, :] = res_t[h * LANES:(h + 1) * LANES, :].astype(BF16)
            outs[0][0, h, LANES:V_ROWS, :] = ones
        return
    res_ref = outs.pop()

    @pl.when(pl.program_id(0) == 0)
    def _():
        res_ref[...] = jnp.zeros_like(res_ref)

    for p in range(res_ref.shape[1] // MXU_DIM):
        r = res_ref[:, p * MXU_DIM:(p + 1) * MXU_DIM]
        if mode == "g":
            outs[0][0, :, p * MXU_DIM:(p + 1) * MXU_DIM] = _silu(r).astype(BF16)
            continue
        ms = _dot((r * r).astype(BF16), gmat_ref[...])
        rn = r * lax.rsqrt(ms + EPS) * gain_ref[:, p * MXU_DIM:(p + 1) * MXU_DIM]
        for hh in range(MXU_DIM // LANES):
            xh = rn[:, hh * LANES:(hh + 1) * LANES]
            if rope:
                xh = (xh * cos_ref[...] + pltpu.roll(xh, LANES - ROPE_FREQS, 1) * sa_ref[...]
                      + pltpu.roll(xh, ROPE_FREQS, 1) * sb_ref[...])
            outs[0][0, p * (MXU_DIM // LANES) + hh] = xh.astype(BF16)
    res_ref[...] = _dot(h_ref[0], w_ref[...])


def _att_proj(h1, w, *, mode, gain=None, gmat=None, rope_tabs=None):
    nb, l, d = h1.shape
    n = w.shape[0] if mode == "v" else w.shape[1]
    tr = min(1024, l)
    nheads = n // LANES
    rope = rope_tabs is not None
    nt = l // tr
    ntiles = nb * nt
    lag = 0 if mode == "v" else 1

    def cur(j):
        t = jnp.minimum(j, ntiles - 1)
        return t // nt, t % nt

    def done(j):
        t = jnp.maximum(j - lag, 0)
        return t // nt, t % nt

    const = lambda j: (0, 0)
    args = [h1, w]
    in_specs = [pl.BlockSpec((1, tr, d), lambda j: (*cur(j), 0)), _resident(w.shape)]
    scratch = [pltpu.VMEM((tr, n), F32)] if lag else []
    if mode in ("q", "k"):
        args += [gain, gmat]
        in_specs += [pl.BlockSpec((1, n), const), pl.BlockSpec((MXU_DIM, MXU_DIM), const)]
        if rope:
            args += list(rope_tabs)
            in_specs += [pl.BlockSpec((tr, LANES), lambda j: (done(j)[1], 0))] * 3
    if mode == "g":
        out_specs = [pl.BlockSpec((1, tr, n), lambda j: (*done(j), 0))]
        out_shape = [jax.ShapeDtypeStruct((nb, l, n), BF16)]
    elif mode == "v":
        out_specs = [pl.BlockSpec((1, nheads, V_ROWS, tr), lambda j: (done(j)[0], 0, 0, done(j)[1]))]
        out_shape = [jax.ShapeDtypeStruct((nb, nheads, V_ROWS, l), BF16)]
    else:
        hspec = pl.BlockSpec((1, nheads, tr, LANES), lambda j: (done(j)[0], 0, done(j)[1], 0))
        hshape = jax.ShapeDtypeStruct((nb, nheads, l, LANES), BF16)
        out_specs, out_shape = [hspec], [hshape]
    kern = functools.partial(_att_proj_kernel, mode=mode, rope=rope, nheads=nheads)
    return pl.pallas_call(
        kern,
        grid=(ntiles + lag,),
        in_specs=in_specs,
        out_specs=out_specs,
        out_shape=out_shape,
        scratch_shapes=scratch,
        compiler_params=_params(("arbitrary",)),
        name="att_proj_" + mode + ("_rope" if rope else ""),
    )(*args)


def _attn_kernel(q_ref, kl_ref, kc_ref, vtl_ref, vtc_ref, lv_ref, sg_ref, o_ref, st_ref, m_ref, p_ref, *,
                 lam_init, tq):
    @pl.when(pl.program_id(0) == 0)
    def _():
        st_ref[...] = jnp.zeros_like(st_ref)
        m_ref[...] = jnp.zeros_like(m_ref)
        p_ref[...] = jnp.ones_like(p_ref)

    lv = lv_ref[...]
    lam = (jnp.exp(jnp.sum(lv[0:1] * lv[1:2], axis=-1, keepdims=True))
           - jnp.exp(jnp.sum(lv[2:3] * lv[3:4], axis=-1, keepdims=True)) + lam_init)
    s = kl_ref.shape[2]
    ot = _dot(vtl_ref[0, 0], p_ref[0:s, :]) + _dot(vtc_ref[0, 0], p_ref[s:, :])
    on = ot[0:LANES] * (1.0 / ot[LANES:LANES + 1])
    od = on[:, 0:tq] - lam * on[:, tq:2 * tq]
    ms = jnp.mean(od * od, axis=0, keepdims=True)
    o_ref[0, 0] = (od * lax.rsqrt(ms + EPS) * sg_ref[...]).T.astype(BF16)

    p_ref[...] = jnp.exp2((st_ref[...] - m_ref[...]).astype(BF16))

    qh = q_ref[0, 0]
    first = lax.broadcasted_iota(jnp.int32, qh.shape, 1) < HEAD_DIM
    zero = jnp.zeros_like(qh)
    q = jnp.concatenate([jnp.where(first, qh, zero), jnp.where(first, zero, qh)], axis=0)
    st_l = lax.dot_general(kl_ref[0, 0], q, _NT, preferred_element_type=F32)
    st_c = lax.dot_general(kc_ref[0, 0], q, _NT, preferred_element_type=F32)
    st_ref[0:s, :] = st_l
    st_ref[s:, :] = st_c
    m_ref[...] = jnp.maximum(jnp.max(st_l, axis=0, keepdims=True), jnp.max(st_c, axis=0, keepdims=True))


def _attention(qh, k_lat, k_ctx, vt_lat, vt_ctx, lam_vecs, subln_t, *, lam_init):
    nb, nh, s, _ = qh.shape
    clen = k_ctx.shape[2]
    ltot = s + clen
    tq = subln_t.shape[1]
    nq = s // tq
    ntiles = nb * nh * nq

    def tile(t):
        return t // (nh * nq), (t // nq) % nh, t % nq

    def q_map(i):
        b, h, qi = tile(jnp.minimum(i, ntiles - 1))
        return b, h, qi, 0

    def k_map(i):
        b, h, _ = tile(jnp.minimum(i, ntiles - 1))
        return b, h, 0, 0

    def v_map(i):
        b, h, _ = tile(jnp.maximum(i - 2, 0))
        return b, h, 0, 0

    def o_map(i):
        b, h, qi = tile(jnp.maximum(i - 2, 0))
        return b, h, qi, 0

    qspec = pl.BlockSpec((1, 1, tq, LANES), q_map)
    return pl.pallas_call(
        functools.partial(_attn_kernel, lam_init=lam_init, tq=tq),
        grid=(ntiles + 2,),
        in_specs=[qspec,
                  pl.BlockSpec((1, 1, s, LANES), k_map),
                  pl.BlockSpec((1, 1, clen, LANES), k_map),
                  pl.BlockSpec((1, 1, V_ROWS, s), v_map),
                  pl.BlockSpec((1, 1, V_ROWS, clen), v_map),
                  pl.BlockSpec((4, HEAD_DIM), lambda i: (0, 0)),
                  pl.BlockSpec((LANES, tq), lambda i: (0, 0))],
        out_specs=pl.BlockSpec((1, 1, tq, LANES), o_map),
        out_shape=jax.ShapeDtypeStruct((nb, nh, s, LANES), BF16),
        scratch_shapes=[pltpu.VMEM((ltot, 2 * tq), F32), pltpu.VMEM((1, 2 * tq), F32),
                        pltpu.VMEM((ltot, 2 * tq), BF16)],
        compiler_params=_params(("arbitrary",)),
        name="diff_attention",
    )(qh, k_lat, k_ctx, vt_lat, vt_ctx, lam_vecs, subln_t)


def _att_out_kernel(o_ref, gs_ref, w_ref, x_ref, gate_ref, out_ref, *, nheads):
    o = jnp.concatenate([o_ref[0, h] for h in range(nheads)], axis=1)
    z = (o.astype(F32) * gs_ref[0].astype(F32)).astype(BF16)
    out_ref[0] = x_ref[0] + gate_ref[0] * _dot(z, w_ref[...])


def _att_out(o_heads, gs, w_out, x1, gate):
    nb, nh, s, _ = o_heads.shape
    d = w_out.shape[1]
    wd = w_out.shape[0]
    tr = min(512, s)
    return pl.pallas_call(
        functools.partial(_att_out_kernel, nheads=nh),
        grid=(nb, s // tr),
        in_specs=[
            pl.BlockSpec((1, nh, tr, LANES), lambda b, i: (b, 0, i, 0)),
            pl.BlockSpec((1, tr, wd), lambda b, i: (b, i, 0)),
            _resident((wd, d)),
            pl.BlockSpec((1, tr, d), lambda b, i: (b, i, 0)),
            pl.BlockSpec((1, 1, d), lambda b, i: (b, 0, 0)),
        ],
        out_specs=pl.BlockSpec((1, tr, d), lambda b, i: (b, i, 0)),
        out_shape=jax.ShapeDtypeStruct((nb, s, d), F32),
        compiler_params=_params(("parallel", "parallel")),
        name="att_out",
    )(o_heads, gs, w_out, x1, gate)


def _gate_weights(gw, gb):
    w = 0.5 * jnp.concatenate([gw[0], gw[1]], axis=-1)
    b = 0.5 * jnp.concatenate([gb[0], gb[1]], axis=-1)
    b_hi = b.astype(BF16)
    b_lo = (b - b_hi.astype(F32)).astype(BF16)
    pad = jnp.zeros((w.shape[0], LANES - 2, 2 * LANES), BF16)
    return jnp.concatenate([w.astype(BF16), b_hi[:, None, :], b_lo[:, None, :], pad], axis=1)


def _rope_tables(s):
    t = jnp.arange(s)
    pos = jnp.stack([(t // GRID_W).astype(F32), (t % GRID_W).astype(F32)], axis=1)
    inv = ROPE_BASE ** (-jnp.arange(ROPE_FREQS, dtype=F32) / ROPE_FREQS)
    ang = pos[:, :, None] * inv
    lane = jnp.arange(LANES)
    dd = lane % HEAD_DIM
    axis, half, f = dd // (2 * ROPE_FREQS), (dd % (2 * ROPE_FREQS)) // ROPE_FREQS, dd % ROPE_FREQS
    cos = jnp.cos(ang)[:, axis, f]
    sin = jnp.sin(ang)[:, axis, f]
    return cos, jnp.where(half == 0, -sin, 0.0), jnp.where(half == 1, sin, 0.0)


def kernel(x, c, ctx, c_ctx, mod_w, mod_b, norm_g, lru_w_in, lru_conv_w, lru_conv_b, lru_gate_w, lru_gate_b,
           lru_lambda, lru_w_out, att_w_in, att_q_norm, att_k_norm, att_lambda, att_subln, att_w_out):
    nb, s, d = x.shape
    clen = ctx.shape[1]
    aw = att_w_out.shape[1]
    assert lru_gate_w.shape[-1] == LANES and d % MXU_DIM == 0 and nb % 8 == 0

    rows = ((nb + 1 + 7) // 8) * 8
    cs = jnp.zeros((rows, d), F32).at[:nb].set(c).at[nb].set(c_ctx)
    mod = _modulation(cs, mod_w, mod_b)

    def split(i):
        lat = [mod[i, :nb, k * d:(k + 1) * d].reshape(nb, 1, d) for k in range(3)]
        cx = [jnp.broadcast_to(mod[i, nb:nb + 1, k * d:(k + 1) * d], (nb, d)).reshape(nb, 1, d) for k in range(3)]
        return lat, cx

    (shift0, scale0, gate0), (cshift0, cscale0, cgate0) = split(0)
    (shift1, scale1, gate1), (cshift1, cscale1, _) = split(1)

    w_in0 = lru_w_in[0].astype(BF16)
    ltot = s + clen
    both = lambda a, b: jnp.stack([a.reshape(nb, d), b.reshape(nb, d)])
    u_tm, gs_tm = _lru_in(ctx, x, both(cshift0, shift0), both(cscale0, scale0), norm_g[0:1], w_in0)
    scan = lambda dr, **kw: _lru_scan(u_tm, lru_conv_w[0], lru_conv_b[0:1],
                                      _gate_weights(lru_gate_w[0, dr], lru_gate_b[0, dr]), lru_lambda[0, dr:dr + 1],
                                      nb=nb, c=clen, reverse=bool(dr), **kw)
    hs_f, xc_tm = scan(0)
    hs = [hs_f, scan(1, xc=xc_tm)]
    w_out0 = lru_w_out[0].astype(BF16)
    g1 = norm_g[1:2]
    _, h1_ctx = _lru_out(hs[0], hs[1], gs_tm, w_out0, ctx, cgate0, g1, cshift1, cscale1, row_off=0)
    x1, h1_lat = _lru_out(hs[0], hs[1], gs_tm, w_out0, x, gate0, g1, shift1, scale1, row_off=clen)

    lam_init = 0.8 - 0.6 * math.exp(-0.3 * 1)
    wq, wk, wv, wg = (att_w_in[0][:, k * aw:(k + 1) * aw].astype(BF16) for k in range(4))
    gmat = jnp.kron(jnp.eye(MXU_DIM // HEAD_DIM, dtype=F32),
                    jnp.full((HEAD_DIM, HEAD_DIM), 1.0 / HEAD_DIM, F32)).astype(BF16)
    reps = aw // HEAD_DIM
    q_gain = jnp.tile(att_q_norm[0] * (HEAD_DIM ** -0.5 * math.log2(math.e)), reps)[None, :]
    k_gain = jnp.tile(att_k_norm[0], reps)[None, :]
    tabs = _rope_tables(s)
    qh, = _att_proj(h1_lat, wq, mode="q", gain=q_gain, gmat=gmat, rope_tabs=tabs)
    k_lat, = _att_proj(h1_lat, wk, mode="k", gain=k_gain, gmat=gmat, rope_tabs=tabs)
    k_ctx, = _att_proj(h1_ctx, wk, mode="k", gain=k_gain, gmat=gmat)
    vt_lat, = _att_proj(h1_lat, wv.T, mode="v")
    vt_ctx, = _att_proj(h1_ctx, wv.T, mode="v")
    gsa, = _att_proj(h1_lat, wg, mode="g")
    subln_t = jnp.broadcast_to((att_subln[0] * (1.0 - lam_init))[:, None], (LANES, min(ATT_TQ, s)))
    o_heads = _attention(qh, k_lat, k_ctx, vt_lat, vt_ctx, att_lambda[0], subln_t, lam_init=lam_init)
    return _att_out(o_heads, gsa, att_w_out[0].astype(BF16), x1, gate1)
```
